```python
import jax, jax.numpy as jnp
from jax import lax
import numpy as np

D_MODEL = 1024
BATCH = 8
SEQ = 4096
DEPTH = 2

N_A_LAYERS = DEPTH // 2
N_B_LAYERS = DEPTH - N_A_LAYERS
HEAD_DIM = 64
MIX_WIDTH = D_MODEL
MEM_HEADS = 4
MEM_WIDTH = MEM_HEADS * HEAD_DIM
N_MEM = 256
LRU_WIDTH = MIX_WIDTH - MEM_WIDTH
LRU_BLOCKS = 6
LRU_BLOCK = LRU_WIDTH // LRU_BLOCKS
CONV_WIDTH = 4
LRU_C = 8.0
SWA_Q_HEADS = (MIX_WIDTH - MEM_WIDTH) // HEAD_DIM
SWA_KV_HEADS = 4
SWA_GROUP = SWA_Q_HEADS // SWA_KV_HEADS
KV_WIDTH = SWA_KV_HEADS * HEAD_DIM
WINDOW = 128
BLOCK = 128
ROPE_DIM = HEAD_DIM // 4
ROPE_THETA = 500000.0
D_FF = -(-8 * D_MODEL // (3 * 256)) * 256
A_IN_WIDTH = 2 * LRU_WIDTH + MEM_WIDTH
B_IN_WIDTH = SWA_Q_HEADS * HEAD_DIM + MEM_WIDTH
EPS = 1e-6
NEG_INF = -1e30

kernel_name = "hybrid_rglru_swa_sink_yoco"


def rms_norm(x, g):
    xf = x.astype(jnp.float32)
    y = xf * lax.rsqrt(jnp.mean(xf * xf, axis=-1, keepdims=True) + EPS)
    return (y * g.astype(jnp.float32)).astype(x.dtype)


def rope_tables(seq):
    inv_freq = 1.0 / (ROPE_THETA ** (jnp.arange(0, ROPE_DIM, 2, dtype=jnp.float32) / ROPE_DIM))
    ang = jnp.arange(seq, dtype=jnp.float32)[:, None] * inv_freq[None, :]
    return jnp.cos(ang), jnp.sin(ang)


def apply_partial_rope(x, cos, sin):
    half = ROPE_DIM // 2
    xr = x[..., :ROPE_DIM].astype(jnp.float32)
    x1, x2 = xr[..., :half], xr[..., half:]
    c = cos[None, :, None, :]
    s = sin[None, :, None, :]
    rot = jnp.concatenate([x1 * c - x2 * s, x2 * c + x1 * s], axis=-1).astype(x.dtype)
    return jnp.concatenate([rot, x[..., ROPE_DIM:]], axis=-1)


def causal_depthwise_conv(x, w, b):
    c = x.shape[-1]
    y = lax.conv_general_dilated(
        x, w[:, None, :].astype(x.dtype), window_strides=(1,),
        padding=[(CONV_WIDTH - 1, 0)], dimension_numbers=("NWC", "WIO", "NWC"),
        feature_group_count=c)
    return y + b


def _linear_recurrence_combine(left, right):
    a1, b1 = left
    a2, b2 = right
    return a1 * a2, a2 * b1 + b2


def rg_lru(x, w_r, b_r, w_i, b_i, lam):
    bsz, seq, width = x.shape
    xb = x.reshape(bsz, seq, LRU_BLOCKS, LRU_BLOCK)
    r = jax.nn.sigmoid(jnp.einsum("bshi,hij->bshj", xb, w_r)
                       + b_r.reshape(LRU_BLOCKS, LRU_BLOCK)).reshape(bsz, seq, width)
    i = jax.nn.sigmoid(jnp.einsum("bshi,hij->bshj", xb, w_i)
                       + b_i.reshape(LRU_BLOCKS, LRU_BLOCK)).reshape(bsz, seq, width)
    log_a = LRU_C * r.astype(jnp.float32) * jax.nn.log_sigmoid(lam.astype(jnp.float32))
    a = jnp.exp(log_a)
    mult = jnp.sqrt(jnp.maximum(1.0 - jnp.exp(2.0 * log_a), 0.0))
    first = (jnp.arange(seq) == 0)[None, :, None]
    mult = jnp.where(first, 1.0, mult)
    u = mult * (i * x).astype(jnp.float32)
    _, h = lax.associative_scan(_linear_recurrence_combine, (a, u), axis=1)
    return h.astype(x.dtype)


def sliding_window_attention_with_sinks(q, k, v, sinks):
    bsz, seq = q.shape[0], q.shape[1]
    nb = seq // BLOCK
    qb = q.reshape(bsz, nb, BLOCK, SWA_KV_HEADS, SWA_GROUP, HEAD_DIM)

    def band(t):
        cur = t.reshape(bsz, nb, BLOCK, SWA_KV_HEADS, HEAD_DIM)
        prev = jnp.concatenate([jnp.zeros_like(cur[:, :1]), cur[:, :-1]], axis=1)
        return jnp.concatenate([prev, cur], axis=2)

    kb, vb = band(k), band(v)
    scores = jnp.einsum("bnqkgd,bnckd->bnkgqc", qb, kb).astype(jnp.float32) * (HEAD_DIM ** -0.5)
    qi = jnp.arange(BLOCK)[:, None] + BLOCK
    ci = jnp.arange(2 * BLOCK)[None, :]
    rel = qi - ci
    in_window = (rel >= 0) & (rel < WINDOW)
    has_prev = (jnp.arange(nb) > 0)[:, None, None] | (ci >= BLOCK)[None]
    mask = in_window[None] & has_prev
    scores = jnp.where(mask[None, :, None, None], scores, NEG_INF)
    sink = sinks.astype(jnp.float32).reshape(1, 1, SWA_KV_HEADS, SWA_GROUP, 1, 1)
    m = jnp.maximum(jnp.max(scores, axis=-1, keepdims=True), sink)
    e = jnp.exp(scores - m)
    probs = e / (jnp.sum(e, axis=-1, keepdims=True) + jnp.exp(sink - m))
    out = jnp.einsum("bnkgqc,bnckd->bnqkgd", probs.astype(v.dtype), vb)
    return out.reshape(bsz, seq, SWA_Q_HEADS * HEAD_DIM)


def memory_attention(q, mk, mv):
    bsz, seq = q.shape[0], q.shape[1]
    s = jnp.einsum("bshd,bmhd->bhsm", q, mk).astype(jnp.float32) * (HEAD_DIM ** -0.5)
    p = jax.nn.softmax(s, axis=-1).astype(mv.dtype)
    o = jnp.einsum("bhsm,bmhd->bshd", p, mv)
    return o.reshape(bsz, seq, MEM_WIDTH)


def swiglu_ffn(x, w_in, w_out):
    gu = x @ w_in
    g, u = gu[..., :D_FF], gu[..., D_FF:]
    return (jax.nn.silu(g) * u) @ w_out


def setup_inputs(seed: int = 0) -> dict:
    key = jax.random.key(seed)
    ks = jax.random.split(key, 24)

    def nrm(k, shape, scale):
        return jax.random.normal(k, shape, dtype=jnp.float32) * scale

    def gain(k, shape):
        return 1.0 + 0.05 * jax.random.normal(k, shape, dtype=jnp.float32)

    u = jax.random.uniform(ks[14], (N_A_LAYERS, LRU_WIDTH), dtype=jnp.float32,
                           minval=0.81, maxval=0.998)
    a0 = jnp.sqrt(u)
    lru_lambda = jnp.log(a0) - jnp.log1p(-a0)
    return {
        "x": nrm(ks[0], (BATCH, SEQ, D_MODEL), 1.0),
        "mem": nrm(ks[1], (BATCH, N_MEM, D_MODEL), 1.0),
        "norm_mix_pre": gain(ks[2], (DEPTH, D_MODEL)),
        "norm_mix_post": gain(ks[3], (DEPTH, D_MODEL)),
        "norm_ffn_pre": gain(ks[4], (DEPTH, D_MODEL)),
        "norm_ffn_post": gain(ks[5], (DEPTH, D_MODEL)),
        "mem_norm": gain(ks[6], (D_MODEL,)),
        "w_mem_kv": nrm(ks[7], (DEPTH, D_MODEL, 2 * MEM_WIDTH), D_MODEL ** -0.5),
        "w_in_a": nrm(ks[8], (N_A_LAYERS, D_MODEL, A_IN_WIDTH), D_MODEL ** -0.5),
        "conv_w": nrm(ks[9], (N_A_LAYERS, CONV_WIDTH, LRU_WIDTH), CONV_WIDTH ** -0.5),
        "conv_b": nrm(ks[10], (N_A_LAYERS, LRU_WIDTH), 0.02),
        "w_gate_r": nrm(ks[11], (N_A_LAYERS, LRU_BLOCKS, LRU_BLOCK, LRU_BLOCK), LRU_BLOCK ** -0.5),
        "b_gate_r": nrm(ks[12], (N_A_LAYERS, LRU_WIDTH), 0.02),
        "w_gate_i": nrm(ks[13], (N_A_LAYERS, LRU_BLOCKS, LRU_BLOCK, LRU_BLOCK), LRU_BLOCK ** -0.5),
        "b_gate_i": nrm(ks[15], (N_A_LAYERS, LRU_WIDTH), 0.02),
        "lru_lambda": lru_lambda,
        "norm_kv": gain(ks[16], (D_MODEL,)),
        "w_kv_shared": nrm(ks[17], (D_MODEL, 2 * KV_WIDTH), D_MODEL ** -0.5),
        "w_in_b": nrm(ks[18], (N_B_LAYERS, D_MODEL, B_IN_WIDTH), D_MODEL ** -0.5),
        "sinks": nrm(ks[19], (N_B_LAYERS, SWA_Q_HEADS), 1.0),
        "w_out": nrm(ks[20], (DEPTH, MIX_WIDTH, D_MODEL), MIX_WIDTH ** -0.5),
        "w_ffn_in": nrm(ks[21], (DEPTH, D_MODEL, 2 * D_FF), D_MODEL ** -0.5),
        "w_ffn_out": nrm(ks[22], (DEPTH, D_FF, D_MODEL), D_FF ** -0.5),
    }


def reference(x, mem, norm_mix_pre, norm_mix_post, norm_ffn_pre, norm_ffn_post, mem_norm,
              w_mem_kv, w_in_a, conv_w, conv_b, w_gate_r, b_gate_r, w_gate_i, b_gate_i,
              lru_lambda, norm_kv, w_kv_shared, w_in_b, sinks, w_out, w_ffn_in, w_ffn_out):
    bsz, seq = x.shape[0], x.shape[1]
    n_mem = mem.shape[1]
    cos, sin = rope_tables(seq)
    mem_n = rms_norm(mem, mem_norm)
    h = x
    shared_k = None
    shared_v = None
    for l in range(DEPTH):
        hn = rms_norm(h, norm_mix_pre[l])
        mkv = mem_n @ w_mem_kv[l]
        mk = mkv[..., :MEM_WIDTH].reshape(bsz, n_mem, MEM_HEADS, HEAD_DIM)
        mv = mkv[..., MEM_WIDTH:].reshape(bsz, n_mem, MEM_HEADS, HEAD_DIM)
        if l < N_A_LAYERS:
            a = l
            proj = hn @ w_in_a[a]
            xr = proj[..., :LRU_WIDTH]
            gate = proj[..., LRU_WIDTH:2 * LRU_WIDTH]
            mq = proj[..., 2 * LRU_WIDTH:].reshape(bsz, seq, MEM_HEADS, HEAD_DIM)
            xr = causal_depthwise_conv(xr, conv_w[a], conv_b[a])
            y = rg_lru(xr, w_gate_r[a], b_gate_r[a], w_gate_i[a], b_gate_i[a], lru_lambda[a])
            y = y * jax.nn.gelu(gate)
        else:
            b = l - N_A_LAYERS
            proj = hn @ w_in_b[b]
            q = proj[..., :SWA_Q_HEADS * HEAD_DIM].reshape(bsz, seq, SWA_Q_HEADS, HEAD_DIM)
            mq = proj[..., SWA_Q_HEADS * HEAD_DIM:].reshape(bsz, seq, MEM_HEADS, HEAD_DIM)
            q = apply_partial_rope(q, cos, sin)
            y = sliding_window_attention_with_sinks(q, shared_k, shared_v, sinks[b])
        m_out = memory_attention(mq, mk, mv)
        mixed = jnp.concatenate([y, m_out], axis=-1) @ w_out[l]
        h = h + rms_norm(mixed, norm_mix_post[l])
        f = swiglu_ffn(rms_norm(h, norm_ffn_pre[l]), w_ffn_in[l], w_ffn_out[l])
        h = h + rms_norm(f, norm_ffn_post[l])
        if l == N_A_LAYERS - 1:
            kv = rms_norm(h, norm_kv) @ w_kv_shared
            shared_k = apply_partial_rope(
                kv[..., :KV_WIDTH].reshape(bsz, seq, SWA_KV_HEADS, HEAD_DIM), cos, sin)
            shared_v = kv[..., KV_WIDTH:].reshape(bsz, seq, SWA_KV_HEADS, HEAD_DIM)
    return h
```

```python
import functools
import math

import jax
import jax.numpy as jnp
from jax import lax
from jax.experimental import pallas as pl
from jax.experimental.pallas import tpu as pltpu

D_MODEL = 1024
BATCH = 8
SEQ = 4096
HEAD_DIM = 64
MEM_HEADS = 4
MEM_WIDTH = MEM_HEADS * HEAD_DIM
N_MEM = 256
LRU_WIDTH = D_MODEL - MEM_WIDTH
LRU_BLOCKS = 6
LRU_BLOCK = LRU_WIDTH // LRU_BLOCKS
CONV_WIDTH = 4
LRU_C = 8.0
SWA_Q_HEADS = 12
SWA_KV_HEADS = 4
SWA_GROUP = SWA_Q_HEADS // SWA_KV_HEADS
KV_WIDTH = SWA_KV_HEADS * HEAD_DIM
WINDOW = 128
ROPE_DIM = HEAD_DIM // 4
ROPE_HALF = ROPE_DIM // 2
ROPE_THETA = 500000.0
D_FF = 2816
EPS = 1e-6
NEG_INF = -1e30
ATTN_SCALE = HEAD_DIM ** -0.5

LANES = 128
SUBLANES = 8
VMEM_LIMIT = 56 * 1024 * 1024

TS_A = 64
PITCH_A = TS_A + SUBLANES
TM_FFN = 512
FF_CHUNK = 256
TQ_B = 512

BF16 = jnp.bfloat16
F32 = jnp.float32


def _rms(x, g):
    ms = jnp.mean(x * x, axis=-1, keepdims=True)
    return x * lax.rsqrt(ms + EPS) * g


def _dot(a, b):
    return jnp.dot(a, b, preferred_element_type=F32)


def _dot_nt(a, b):
    return lax.dot_general(a, b, (((1,), (1,)), ((), ())), preferred_element_type=F32)


def _gelu_tanh(x):
    return 0.5 * x * (1.0 + jnp.tanh(math.sqrt(2.0 / math.pi) * (x + 0.044715 * (x * x * x))))


def _rope(x, cos, sin):
    lane = lax.broadcasted_iota(jnp.int32, x.shape, 1) & (HEAD_DIM - 1)
    partner = jnp.where(lane < ROPE_HALF,
                        pltpu.roll(x, LANES - ROPE_HALF, 1),
                        pltpu.roll(x, ROPE_HALF, 1))
    return x * cos + partner * sin


def _mem_attention(q, mkv_ref, b, head):
    k = mkv_ref[b, :, head * HEAD_DIM:(head + 1) * HEAD_DIM]
    v = mkv_ref[b, :, MEM_WIDTH + head * HEAD_DIM:MEM_WIDTH + (head + 1) * HEAD_DIM]
    s = _dot_nt(q, k)
    m = jnp.max(s, axis=-1, keepdims=True)
    e = jnp.exp(s - m)
    den = jnp.sum(e, axis=-1, keepdims=True)
    return _dot(e.astype(BF16), v) / den


def _memkv_kernel(mem_ref, g_ref, w_ref, o_ref):
    mem_n = _rms(mem_ref[0], g_ref[...]).astype(BF16)
    for l in range(2):
        o_ref[l, 0] = _dot(mem_n, w_ref[l]).astype(BF16)


def _memkv(mem, mem_norm, w_mem_kv):
    return pl.pallas_call(
        _memkv_kernel,
        grid=(BATCH,),
        in_specs=[
            pl.BlockSpec((1, N_MEM, D_MODEL), lambda b: (b, 0, 0)),
            pl.BlockSpec((1, D_MODEL), lambda b: (0, 0)),
            pl.BlockSpec((2, D_MODEL, 2 * MEM_WIDTH), lambda b: (0, 0, 0)),
        ],
        out_specs=pl.BlockSpec((2, 1, N_MEM, 2 * MEM_WIDTH), lambda b: (0, b, 0, 0)),
        out_shape=jax.ShapeDtypeStruct((2, BATCH, N_MEM, 2 * MEM_WIDTH), BF16),
        compiler_params=pltpu.CompilerParams(dimension_semantics=("arbitrary",),
                                             vmem_limit_bytes=VMEM_LIMIT),
        name="memkv",
    )(mem, mem_norm.reshape(1, D_MODEL), w_mem_kv)


def _mixer_a_kernel(x_ref, gpre_ref, win_ref, convw_ref, convb_ref, wg_ref, bg_ref, lam_ref,
                    mkv_ref, wout_ref, gpost_ref, o_ref,
                    xr_s, xc_s, a_s, u_s, y_s, cat_s, h_s):
    i = pl.program_id(0)
    ts, pitch = TS_A, PITCH_A
    rows = BATCH * ts

    @pl.when(i == 0)
    def _():
        h_s[...] = jnp.zeros_like(h_s)
        xr_s[:, 0:SUBLANES, :] = jnp.zeros((BATCH, SUBLANES, LRU_WIDTH), F32)

    x = x_ref[...].reshape(rows, D_MODEL)
    hn = _rms(x, gpre_ref[...]).astype(BF16)

    xr = _dot(hn, win_ref[:, 0:LRU_WIDTH])
    for b in range(BATCH):
        xr_s[b, SUBLANES:SUBLANES + ts, :] = xr[b * ts:(b + 1) * ts]
    for b in range(BATCH):
        acc = jnp.broadcast_to(convb_ref[...], (ts, LRU_WIDTH))
        for k in range(CONV_WIDTH):
            off = SUBLANES - (CONV_WIDTH - 1) + k
            acc = acc + xr_s[b, off:off + ts, :] * convw_ref[k:k + 1, :]
        xc_s[b * ts:(b + 1) * ts, :] = acc
        xr_s[b, 0:SUBLANES, :] = xr_s[b, ts:ts + SUBLANES, :]

    lam = lam_ref[...]
    log_sig = jnp.minimum(lam, 0.0) - jnp.log1p(jnp.exp(-jnp.abs(lam)))
    ls_c = LRU_C * log_sig
    tpos = lax.broadcasted_iota(jnp.int32, (rows, 1), 0) & (ts - 1)
    seq_start = jnp.logical_and(tpos == 0, i == 0)
    for c in range(LRU_BLOCKS):
        cs = slice(c * LRU_BLOCK, (c + 1) * LRU_BLOCK)
        xc = xc_s[:, cs]
        g = _dot(xc.astype(BF16), wg_ref[c]) + bg_ref[c]
        r = jax.nn.sigmoid(g[:, :LRU_BLOCK])
        ig = jax.nn.sigmoid(g[:, LRU_BLOCK:])
        a = jnp.exp(r * ls_c[:, cs])
        mult = jnp.sqrt(jnp.maximum(1.0 - a * a, 0.0))
        mult = jnp.where(seq_start, 1.0, mult)
        u = mult * (ig * xc)
        for b in range(BATCH):
            a_s[c, b * pitch:b * pitch + ts, :] = a[b * ts:(b + 1) * ts]
            u_s[c, b * pitch:b * pitch + ts, :] = u[b * ts:(b + 1) * ts]

    def step(t, h):
        out = []
        for c in range(LRU_BLOCKS):
            idx = pl.ds(t, BATCH, stride=pitch)
            hc = a_s[c, idx, :] * h[c] + u_s[c, idx, :]
            y_s[c, idx, :] = hc
            out.append(hc)
        return tuple(out)

    h0 = tuple(h_s[:, c * LRU_BLOCK:(c + 1) * LRU_BLOCK] for c in range(LRU_BLOCKS))
    h_last = lax.fori_loop(0, ts, step, h0, unroll=8)
    for c in range(LRU_BLOCKS):
        h_s[:, c * LRU_BLOCK:(c + 1) * LRU_BLOCK] = h_last[c]

    gate = _gelu_tanh(_dot(hn, win_ref[:, LRU_WIDTH:2 * LRU_WIDTH]))
    for c in range(LRU_BLOCKS):
        cs = slice(c * LRU_BLOCK, (c + 1) * LRU_BLOCK)
        for b in range(BATCH):
            y = y_s[c, b * pitch:b * pitch + ts, :]
            cat_s[b * ts:(b + 1) * ts, cs] = (y * gate[b * ts:(b + 1) * ts, cs]).astype(BF16)

    mq = (_dot(hn, win_ref[:, 2 * LRU_WIDTH:]) * ATTN_SCALE).astype(BF16)
    for b in range(BATCH):
        for hp in range(MEM_HEADS // 2):
            pair = [_mem_attention(mq[b * ts:(b + 1) * ts, h * HEAD_DIM:(h + 1) * HEAD_DIM], mkv_ref, b, h)
                    for h in (2 * hp, 2 * hp + 1)]
            col = LRU_WIDTH + hp * LANES
            cat_s[b * ts:(b + 1) * ts, col:col + LANES] = jnp.concatenate(pair, axis=-1).astype(BF16)

    mixed = _dot(cat_s[...], wout_ref[...])
    o_ref[...] = (x + _rms(mixed, gpost_ref[...])).reshape(BATCH, ts, D_MODEL)


def _const_spec(shape):
    nd = len(shape)
    return pl.BlockSpec(shape, lambda *_: (0,) * nd)


def _mixer_a(x, gpre, w_in, conv_w, conv_b, w_gate, b_gate, lam, mkv, w_out, gpost):
    ts, pitch = TS_A, PITCH_A
    rows = BATCH * ts
    return pl.pallas_call(
        _mixer_a_kernel,
        grid=(SEQ // ts,),
        in_specs=[
            pl.BlockSpec((BATCH, ts, D_MODEL), lambda i: (0, i, 0)),
            _const_spec((1, D_MODEL)),
            _const_spec(w_in.shape),
            _const_spec(conv_w.shape),
            _const_spec((1, LRU_WIDTH)),
            _const_spec(w_gate.shape),
            _const_spec(b_gate.shape),
            _const_spec((1, LRU_WIDTH)),
            pl.BlockSpec((None, BATCH, N_MEM, 2 * MEM_WIDTH), lambda i: (0, 0, 0, 0)),
            _const_spec(w_out.shape),
            _const_spec((1, D_MODEL)),
        ],
        out_specs=pl.BlockSpec((BATCH, ts, D_MODEL), lambda i: (0, i, 0)),
        out_shape=jax.ShapeDtypeStruct((BATCH, SEQ, D_MODEL), F32),
        scratch_shapes=[
            pltpu.VMEM((BATCH, ts + SUBLANES, LRU_WIDTH), F32),
            pltpu.VMEM((rows, LRU_WIDTH), F32),
            pltpu.VMEM((LRU_BLOCKS, BATCH * pitch, LRU_BLOCK), F32),
            pltpu.VMEM((LRU_BLOCKS, BATCH * pitch, LRU_BLOCK), F32),
            pltpu.VMEM((LRU_BLOCKS, BATCH * pitch, LRU_BLOCK), F32),
            pltpu.VMEM((rows, D_MODEL), BF16),
            pltpu.VMEM((BATCH, LRU_WIDTH), F32),
        ],
        compiler_params=pltpu.CompilerParams(dimension_semantics=("arbitrary",),
                                             vmem_limit_bytes=VMEM_LIMIT),
        name="mixer_a",
    )(x, gpre.reshape(1, D_MODEL), w_in, conv_w, conv_b.reshape(1, LRU_WIDTH), w_gate, b_gate,
      lam.reshape(1, LRU_WIDTH), mkv, w_out, gpost.reshape(1, D_MODEL))


def _ffn_body(h_ref, gpre_ref, win_ref, wout_ref, gpost_ref):
    h = h_ref[...]
    hn = _rms(h, gpre_ref[...]).astype(BF16)
    acc = jnp.zeros((h.shape[0], D_MODEL), F32)
    for ck in range(D_FF // FF_CHUNK):
        lo = ck * FF_CHUNK
        g = _dot(hn, win_ref[:, lo:lo + FF_CHUNK])
        u = _dot(hn, win_ref[:, D_FF + lo:D_FF + lo + FF_CHUNK])
        act = (g * jax.nn.sigmoid(g) * u).astype(BF16)
        acc = acc + _dot(act, wout_ref[lo:lo + FF_CHUNK, :])
    return h + _rms(acc, gpost_ref[...])


def _ffn_kernel(h_ref, gpre_ref, win_ref, wout_ref, gpost_ref, o_ref):
    o_ref[...] = _ffn_body(h_ref, gpre_ref, win_ref, wout_ref, gpost_ref)


def _ffn_kv_kernel(h_ref, gpre_ref, win_ref, wout_ref, gpost_ref, gkv_ref, wkv_ref, cos_ref, sin_ref,
                   o_ref, k_ref, v_ref):
    h_new = _ffn_body(h_ref, gpre_ref, win_ref, wout_ref, gpost_ref)
    o_ref[...] = h_new
    kv = _dot(_rms(h_new, gkv_ref[...]).astype(BF16), wkv_ref[...])
    cos, sin = cos_ref[...], sin_ref[...]
    for j in range(KV_WIDTH // LANES):
        k_ref[:, j * LANES:(j + 1) * LANES] = _rope(kv[:, j * LANES:(j + 1) * LANES], cos, sin).astype(BF16)
    v_ref[...] = kv[:, KV_WIDTH:].astype(BF16)


def _ffn(h2d, gpre, w_in, w_out, gpost, kv_args=None):
    tokens = h2d.shape[0]
    tm = TM_FFN
    row_spec = pl.BlockSpec((tm, D_MODEL), lambda i: (i, 0))
    in_specs = [row_spec, _const_spec((1, D_MODEL)), _const_spec(w_in.shape), _const_spec(w_out.shape),
                _const_spec((1, D_MODEL))]
    args = [h2d, gpre.reshape(1, D_MODEL), w_in, w_out, gpost.reshape(1, D_MODEL)]
    params = pltpu.CompilerParams(dimension_semantics=("arbitrary",), vmem_limit_bytes=VMEM_LIMIT)
    if kv_args is None:
        return pl.pallas_call(
            _ffn_kernel, grid=(tokens // tm,), in_specs=in_specs, out_specs=row_spec,
            out_shape=jax.ShapeDtypeStruct((tokens, D_MODEL), F32),
            compiler_params=params, name="ffn",
        )(*args)
    gkv, wkv, cos, sin = kv_args
    pos_blocks = SEQ // tm
    table_spec = pl.BlockSpec((tm, LANES), lambda i: (i % pos_blocks, 0))
    kv_spec = pl.BlockSpec((tm, KV_WIDTH), lambda i: (i, 0))
    return pl.pallas_call(
        _ffn_kv_kernel, grid=(tokens // tm,),
        in_specs=in_specs + [_const_spec((1, D_MODEL)), _const_spec(wkv.shape), table_spec, table_spec],
        out_specs=[row_spec, kv_spec, kv_spec],
        out_shape=[jax.ShapeDtypeStruct((tokens, D_MODEL), F32),
                   jax.ShapeDtypeStruct((tokens, KV_WIDTH), BF16),
                   jax.ShapeDtypeStruct((tokens, KV_WIDTH), BF16)],
        compiler_params=params, name="ffn_kv",
    )(*args, gkv.reshape(1, D_MODEL), wkv, cos, sin)


def _mixer_b_kernel(sinks_ref, x_ref, gpre_ref, win_ref, cos_ref, sin_ref, k_ref, v_ref, mkv_ref,
                    wout_ref, gpost_ref, o_ref, q_s, cat_s):
    j = pl.program_id(1)
    tq = TQ_B
    x = x_ref[0]
    hn = _rms(x, gpre_ref[...]).astype(BF16)

    cos, sin = cos_ref[...], sin_ref[...]
    for c in range(SWA_Q_HEADS * HEAD_DIM // LANES):
        cs = slice(c * LANES, (c + 1) * LANES)
        q = _dot(hn, win_ref[:, cs])
        q_s[:, cs] = (_rope(q, cos, sin) * ATTN_SCALE).astype(BF16)

    qi = lax.broadcasted_iota(jnp.int32, (WINDOW, WINDOW), 0)
    ci = lax.broadcasted_iota(jnp.int32, (WINDOW, WINDOW), 1)
    cur_mask = jnp.concatenate([ci <= qi] * SWA_GROUP, axis=0)
    prev_tri = jnp.concatenate([ci > qi] * SWA_GROUP, axis=0)

    for qb in range(tq // WINDOW):
        start = pl.multiple_of(j * tq + qb * WINDOW, WINDOW)
        pstart = pl.multiple_of(jnp.maximum(start - WINDOW, 0), WINDOW)
        prev_mask = jnp.logical_and(prev_tri, start > 0)
        rs = slice(qb * WINDOW, (qb + 1) * WINDOW)
        for kh in range(SWA_KV_HEADS):
            ks = slice(kh * HEAD_DIM, (kh + 1) * HEAD_DIM)
            kp = k_ref[0, pl.ds(pstart, WINDOW), ks]
            kc = k_ref[0, pl.ds(start, WINDOW), ks]
            vp = v_ref[0, pl.ds(pstart, WINDOW), ks]
            vc = v_ref[0, pl.ds(start, WINDOW), ks]
            heads = [kh * SWA_GROUP + g for g in range(SWA_GROUP)]
            qg = jnp.concatenate([q_s[rs, h * HEAD_DIM:(h + 1) * HEAD_DIM] for h in heads], axis=0)
            sink = jnp.concatenate([jnp.full((WINDOW, 1), sinks_ref[h], F32) for h in heads], axis=0)
            sp = jnp.where(prev_mask, _dot_nt(qg, kp), NEG_INF)
            sc = jnp.where(cur_mask, _dot_nt(qg, kc), NEG_INF)
            m = jnp.maximum(jnp.maximum(jnp.max(sp, axis=-1, keepdims=True),
                                        jnp.max(sc, axis=-1, keepdims=True)), sink)
            ep = jnp.exp(sp - m)
            ec = jnp.exp(sc - m)
            den = (jnp.sum(ep, axis=-1, keepdims=True) + jnp.sum(ec, axis=-1, keepdims=True)
                   + jnp.exp(sink - m))
            o = (_dot(ep.astype(BF16), vp) + _dot(ec.astype(BF16), vc)) / den
            for g, h in enumerate(heads):
                cat_s[rs, h * HEAD_DIM:(h + 1) * HEAD_DIM] = o[g * WINDOW:(g + 1) * WINDOW].astype(BF16)

    b = pl.program_id(0)
    q_width = SWA_Q_HEADS * HEAD_DIM
    mq = (_dot(hn, win_ref[:, q_width:]) * ATTN_SCALE).astype(BF16)
    for hp in range(MEM_HEADS // 2):
        pair = [_mem_attention(mq[:, h * HEAD_DIM:(h + 1) * HEAD_DIM], mkv_ref, b, h)
                for h in (2 * hp, 2 * hp + 1)]
        col = q_width + hp * LANES
        cat_s[:, col:col + LANES] = jnp.concatenate(pair, axis=-1).astype(BF16)

    mixed = _dot(cat_s[...], wout_ref[...])
    o_ref[0] = x + _rms(mixed, gpost_ref[...])


def _mixer_b(h, sinks, gpre, w_in, cos, sin, k, v, mkv, w_out, gpost):
    tq = TQ_B
    grid_spec = pltpu.PrefetchScalarGridSpec(
        num_scalar_prefetch=1,
        grid=(BATCH, SEQ // tq),
        in_specs=[
            pl.BlockSpec((1, tq, D_MODEL), lambda b, j, s: (b, j, 0)),
            pl.BlockSpec((1, D_MODEL), lambda b, j, s: (0, 0)),
            pl.BlockSpec(w_in.shape, lambda b, j, s: (0, 0)),
            pl.BlockSpec((tq, LANES), lambda b, j, s: (j, 0)),
            pl.BlockSpec((tq, LANES), lambda b, j, s: (j, 0)),
            pl.BlockSpec((1, SEQ, KV_WIDTH), lambda b, j, s: (b, 0, 0)),
            pl.BlockSpec((1, SEQ, KV_WIDTH), lambda b, j, s: (b, 0, 0)),
            pl.BlockSpec((None, BATCH, N_MEM, 2 * MEM_WIDTH), lambda b, j, s: (1, 0, 0, 0)),
            pl.BlockSpec(w_out.shape, lambda b, j, s: (0, 0)),
            pl.BlockSpec((1, D_MODEL), lambda b, j, s: (0, 0)),
        ],
        out_specs=pl.BlockSpec((1, tq, D_MODEL), lambda b, j, s: (b, j, 0)),
        scratch_shapes=[
            pltpu.VMEM((tq, SWA_Q_HEADS * HEAD_DIM), BF16),
            pltpu.VMEM((tq, D_MODEL), BF16),
        ],
    )
    return pl.pallas_call(
        _mixer_b_kernel,
        grid_spec=grid_spec,
        out_shape=jax.ShapeDtypeStruct((BATCH, SEQ, D_MODEL), F32),
        compiler_params=pltpu.CompilerParams(dimension_semantics=("arbitrary", "arbitrary"),
                                             vmem_limit_bytes=VMEM_LIMIT),
        name="mixer_b",
    )(sinks, h, gpre.reshape(1, D_MODEL), w_in, cos, sin, k, v, mkv, w_out, gpost.reshape(1, D_MODEL))


def _rope_lane_tables():
    inv_freq = 1.0 / (ROPE_THETA ** (jnp.arange(0, ROPE_DIM, 2, dtype=F32) / ROPE_DIM))
    ang = jnp.arange(SEQ, dtype=F32)[:, None] * inv_freq[None, :]
    c, s = jnp.cos(ang), jnp.sin(ang)
    rest = HEAD_DIM - ROPE_DIM
    cos_h = jnp.concatenate([c, c, jnp.ones((SEQ, rest), F32)], axis=-1)
    sin_h = jnp.concatenate([-s, s, jnp.zeros((SEQ, rest), F32)], axis=-1)
    reps = LANES // HEAD_DIM
    return jnp.tile(cos_h, (1, reps)), jnp.tile(sin_h, (1, reps))


def kernel(x, mem, norm_mix_pre, norm_mix_post, norm_ffn_pre, norm_ffn_post, mem_norm, w_mem_kv, w_in_a,
           conv_w, conv_b, w_gate_r, b_gate_r, w_gate_i, b_gate_i, lru_lambda, norm_kv, w_kv_shared, w_in_b,
           sinks, w_out, w_ffn_in, w_ffn_out):
    tokens = BATCH * SEQ
    cos, sin = _rope_lane_tables()
    w_gate = jnp.concatenate([w_gate_r[0], w_gate_i[0]], axis=-1).astype(BF16)
    b_gate = jnp.concatenate([b_gate_r[0].reshape(LRU_BLOCKS, 1, LRU_BLOCK),
                              b_gate_i[0].reshape(LRU_BLOCKS, 1, LRU_BLOCK)], axis=-1)

    mkv = _memkv(mem, mem_norm, w_mem_kv.astype(BF16))

    h = _mixer_a(x, norm_mix_pre[0], w_in_a[0].astype(BF16), conv_w[0], conv_b[0], w_gate, b_gate,
                 lru_lambda[0], mkv, w_out[0].astype(BF16), norm_mix_post[0])
    h2d, k, v = _ffn(h.reshape(tokens, D_MODEL), norm_ffn_pre[0], w_ffn_in[0].astype(BF16),
                     w_ffn_out[0].astype(BF16), norm_ffn_post[0],
                     kv_args=(norm_kv, w_kv_shared.astype(BF16), cos, sin))

    h = _mixer_b(h2d.reshape(BATCH, SEQ, D_MODEL), sinks[0], norm_mix_pre[1], w_in_b[0].astype(BF16), cos, sin,
                 k.reshape(BATCH, SEQ, KV_WIDTH), v.reshape(BATCH, SEQ, KV_WIDTH), mkv,
                 w_out[1].astype(BF16), norm_mix_post[1])
    h2d = _ffn(h.reshape(tokens, D_MODEL), norm_ffn_pre[1], w_ffn_in[1].astype(BF16),
               w_ffn_out[1].astype(BF16), norm_ffn_post[1])
    return h2d.reshape(BATCH, SEQ, D_MODEL)
```

```python
import functools
import math

import jax
import jax.numpy as jnp
from jax import lax
from jax.experimental import pallas as pl
from jax.experimental.pallas import tpu as pltpu

D_MODEL = 1024
BATCH = 8
SEQ = 4096
HEAD_DIM = 64
MEM_HEADS = 4
MEM_WIDTH = MEM_HEADS * HEAD_DIM
N_MEM = 256
LRU_WIDTH = D_MODEL - MEM_WIDTH
LRU_BLOCKS = 6
LRU_BLOCK = LRU_WIDTH // LRU_BLOCKS
CONV_WIDTH = 4
LRU_C = 8.0
SWA_Q_HEADS = 12
SWA_KV_HEADS = 4
SWA_GROUP = SWA_Q_HEADS // SWA_KV_HEADS
KV_WIDTH = SWA_KV_HEADS * HEAD_DIM
WINDOW = 128
ROPE_DIM = HEAD_DIM // 4
ROPE_HALF = ROPE_DIM // 2
ROPE_THETA = 500000.0
D_FF = 2816
EPS = 1e-6
NEG_INF = -1e30
ATTN_SCALE = HEAD_DIM ** -0.5

LANES = 128
SUBLANES = 8
VMEM_LIMIT = 56 * 1024 * 1024

TS_A = 64
PITCH_A = TS_A + SUBLANES
TM_FFN = 512
FF_CHUNK = 256
TQ_B = 512

BF16 = jnp.bfloat16
F32 = jnp.float32


def _rms(x, g):
    ms = jnp.mean(x * x, axis=-1, keepdims=True)
    return x * lax.rsqrt(ms + EPS) * g


def _dot(a, b):
    return jnp.dot(a, b, preferred_element_type=F32)


def _dot_nt(a, b):
    return lax.dot_general(a, b, (((1,), (1,)), ((), ())), preferred_element_type=F32)


def _gelu_tanh(x):
    return 0.5 * x * (1.0 + jnp.tanh(math.sqrt(2.0 / math.pi) * (x + 0.044715 * (x * x * x))))


def _rope(x, cos, sin):
    lane = lax.broadcasted_iota(jnp.int32, x.shape, 1) & (HEAD_DIM - 1)
    partner = jnp.where(lane < ROPE_HALF,
                        pltpu.roll(x, LANES - ROPE_HALF, 1),
                        pltpu.roll(x, ROPE_HALF, 1))
    return x * cos + partner * sin


def _mem_attention(q, mkv_ref, b, head):
    k = mkv_ref[b, :, head * HEAD_DIM:(head + 1) * HEAD_DIM]
    v = mkv_ref[b, :, MEM_WIDTH + head * HEAD_DIM:MEM_WIDTH + (head + 1) * HEAD_DIM]
    s = _dot_nt(q, k)
    m = jnp.max(s, axis=-1, keepdims=True)
    e = jnp.exp(s - m)
    den = jnp.sum(e, axis=-1, keepdims=True)
    return _dot(e.astype(BF16), v) / den


def _memkv_kernel(mem_ref, g_ref, w_ref, wvt_ref, o_ref, ovt_ref):
    mem_n = _rms(mem_ref[0], g_ref[...]).astype(BF16)
    for l in range(2):
        o_ref[l, 0] = _dot(mem_n, w_ref[l]).astype(BF16)
        ovt_ref[l, 0] = _dot_nt(wvt_ref[l], mem_n).astype(BF16)


def _memkv(mem, mem_norm, w_mem_kv, w_mem_vt):
    return pl.pallas_call(
        _memkv_kernel,
        grid=(BATCH,),
        in_specs=[
            pl.BlockSpec((1, N_MEM, D_MODEL), lambda b: (b, 0, 0)),
            pl.BlockSpec((1, D_MODEL), lambda b: (0, 0)),
            pl.BlockSpec((2, D_MODEL, 2 * MEM_WIDTH), lambda b: (0, 0, 0)),
            pl.BlockSpec((2, MEM_WIDTH, D_MODEL), lambda b: (0, 0, 0)),
        ],
        out_specs=[pl.BlockSpec((2, 1, N_MEM, 2 * MEM_WIDTH), lambda b: (0, b, 0, 0)),
                   pl.BlockSpec((2, 1, MEM_WIDTH, N_MEM), lambda b: (0, b, 0, 0))],
        out_shape=[jax.ShapeDtypeStruct((2, BATCH, N_MEM, 2 * MEM_WIDTH), BF16),
                   jax.ShapeDtypeStruct((2, BATCH, MEM_WIDTH, N_MEM), BF16)],
        compiler_params=pltpu.CompilerParams(dimension_semantics=("arbitrary",),
                                             vmem_limit_bytes=VMEM_LIMIT),
        name="memkv",
    )(mem, mem_norm.reshape(1, D_MODEL), w_mem_kv, w_mem_vt)


def _mixer_a_kernel(x_ref, gpre_ref, win_ref, convw_ref, convb_ref, wg_ref, bg_ref, lam_ref,
                    mkv_ref, wout_ref, gpost_ref, o_ref,
                    xr_s, xc_s, a_s, u_s, y_s, cat_s, h_s):
    i = pl.program_id(0)
    ts, pitch = TS_A, PITCH_A
    rows = BATCH * ts

    @pl.when(i == 0)
    def _():
        h_s[...] = jnp.zeros_like(h_s)
        xr_s[:, 0:SUBLANES, :] = jnp.zeros((BATCH, SUBLANES, LRU_WIDTH), F32)

    x = x_ref[...].reshape(rows, D_MODEL)
    hn = _rms(x, gpre_ref[...]).astype(BF16)

    xr = _dot(hn, win_ref[:, 0:LRU_WIDTH])
    for b in range(BATCH):
        xr_s[b, SUBLANES:SUBLANES + ts, :] = xr[b * ts:(b + 1) * ts]
    for b in range(BATCH):
        acc = jnp.broadcast_to(convb_ref[...], (ts, LRU_WIDTH))
        for k in range(CONV_WIDTH):
            off = SUBLANES - (CONV_WIDTH - 1) + k
            acc = acc + xr_s[b, off:off + ts, :] * convw_ref[k:k + 1, :]
        xc_s[b * ts:(b + 1) * ts, :] = acc
        xr_s[b, 0:SUBLANES, :] = xr_s[b, ts:ts + SUBLANES, :]

    lam = lam_ref[...]
    log_sig = jnp.minimum(lam, 0.0) - jnp.log1p(jnp.exp(-jnp.abs(lam)))
    ls_c = LRU_C * log_sig
    tpos = lax.broadcasted_iota(jnp.int32, (rows, 1), 0) & (ts - 1)
    seq_start = jnp.logical_and(tpos == 0, i == 0)
    for c in range(LRU_BLOCKS):
        cs = slice(c * LRU_BLOCK, (c + 1) * LRU_BLOCK)
        xc = xc_s[:, cs]
        g = _dot(xc.astype(BF16), wg_ref[c]) + bg_ref[c]
        r = jax.nn.sigmoid(g[:, :LRU_BLOCK])
        ig = jax.nn.sigmoid(g[:, LRU_BLOCK:])
        a = jnp.exp(r * ls_c[:, cs])
        mult = jnp.sqrt(jnp.maximum(1.0 - a * a, 0.0))
        mult = jnp.where(seq_start, 1.0, mult)
        u = mult * (ig * xc)
        for b in range(BATCH):
            a_s[c, b * pitch:b * pitch + ts, :] = a[b * ts:(b + 1) * ts]
            u_s[c, b * pitch:b * pitch + ts, :] = u[b * ts:(b + 1) * ts]

    def step(t, h):
        out = []
        for c in range(LRU_BLOCKS):
            idx = pl.ds(t, BATCH, stride=pitch)
            hc = a_s[c, idx, :] * h[c] + u_s[c, idx, :]
            y_s[c, idx, :] = hc
            out.append(hc)
        return tuple(out)

    h0 = tuple(h_s[:, c * LRU_BLOCK:(c + 1) * LRU_BLOCK] for c in range(LRU_BLOCKS))
    h_last = lax.fori_loop(0, ts, step, h0, unroll=8)
    for c in range(LRU_BLOCKS):
        h_s[:, c * LRU_BLOCK:(c + 1) * LRU_BLOCK] = h_last[c]

    gate = _gelu_tanh(_dot(hn, win_ref[:, LRU_WIDTH:2 * LRU_WIDTH]))
    for c in range(LRU_BLOCKS):
        cs = slice(c * LRU_BLOCK, (c + 1) * LRU_BLOCK)
        for b in range(BATCH):
            y = y_s[c, b * pitch:b * pitch + ts, :]
            cat_s[b * ts:(b + 1) * ts, cs] = (y * gate[b * ts:(b + 1) * ts, cs]).astype(BF16)

    mq = (_dot(hn, win_ref[:, 2 * LRU_WIDTH:]) * ATTN_SCALE).astype(BF16)
    for b in range(BATCH):
        for hp in range(MEM_HEADS // 2):
            pair = [_mem_attention(mq[b * ts:(b + 1) * ts, h * HEAD_DIM:(h + 1) * HEAD_DIM], mkv_ref, b, h)
                    for h in (2 * hp, 2 * hp + 1)]
            col = LRU_WIDTH + hp * LANES
            cat_s[b * ts:(b + 1) * ts, col:col + LANES] = jnp.concatenate(pair, axis=-1).astype(BF16)

    mixed = _dot(cat_s[...], wout_ref[...])
    o_ref[...] = (x + _rms(mixed, gpost_ref[...])).reshape(BATCH, ts, D_MODEL)


def _const_spec(shape):
    nd = len(shape)
    return pl.BlockSpec(shape, lambda *_: (0,) * nd)


def _mixer_a(x, gpre, w_in, conv_w, conv_b, w_gate, b_gate, lam, mkv, w_out, gpost):
    ts, pitch = TS_A, PITCH_A
    rows = BATCH * ts
    return pl.pallas_call(
        _mixer_a_kernel,
        grid=(SEQ // ts,),
        in_specs=[
            pl.BlockSpec((BATCH, ts, D_MODEL), lambda i: (0, i, 0)),
            _const_spec((1, D_MODEL)),
            _const_spec(w_in.shape),
            _const_spec(conv_w.shape),
            _const_spec((1, LRU_WIDTH)),
            _const_spec(w_gate.shape),
            _const_spec(b_gate.shape),
            _const_spec((1, LRU_WIDTH)),
            pl.BlockSpec((None, BATCH, N_MEM, 2 * MEM_WIDTH), lambda i: (0, 0, 0, 0)),
            _const_spec(w_out.shape),
            _const_spec((1, D_MODEL)),
        ],
        out_specs=pl.BlockSpec((BATCH, ts, D_MODEL), lambda i: (0, i, 0)),
        out_shape=jax.ShapeDtypeStruct((BATCH, SEQ, D_MODEL), F32),
        scratch_shapes=[
            pltpu.VMEM((BATCH, ts + SUBLANES, LRU_WIDTH), F32),
            pltpu.VMEM((rows, LRU_WIDTH), F32),
            pltpu.VMEM((LRU_BLOCKS, BATCH * pitch, LRU_BLOCK), F32),
            pltpu.VMEM((LRU_BLOCKS, BATCH * pitch, LRU_BLOCK), F32),
            pltpu.VMEM((LRU_BLOCKS, BATCH * pitch, LRU_BLOCK), F32),
            pltpu.VMEM((rows, D_MODEL), BF16),
            pltpu.VMEM((BATCH, LRU_WIDTH), F32),
        ],
        compiler_params=pltpu.CompilerParams(dimension_semantics=("arbitrary",),
                                             vmem_limit_bytes=VMEM_LIMIT),
        name="mixer_a",
    )(x, gpre.reshape(1, D_MODEL), w_in, conv_w, conv_b.reshape(1, LRU_WIDTH), w_gate, b_gate,
      lam.reshape(1, LRU_WIDTH), mkv, w_out, gpost.reshape(1, D_MODEL))


def _ffn_body(h_ref, gpre_ref, win_ref, wout_ref, gpost_ref):
    h = h_ref[...]
    hn = _rms(h, gpre_ref[...]).astype(BF16)
    acc = jnp.zeros((h.shape[0], D_MODEL), F32)
    for ck in range(D_FF // FF_CHUNK):
        lo = ck * FF_CHUNK
        g = _dot(hn, win_ref[:, lo:lo + FF_CHUNK])
        u = _dot(hn, win_ref[:, D_FF + lo:D_FF + lo + FF_CHUNK])
        act = (g * jax.nn.sigmoid(g) * u).astype(BF16)
        acc = acc + _dot(act, wout_ref[lo:lo + FF_CHUNK, :])
    return h + _rms(acc, gpost_ref[...])


def _ffn_kernel(h_ref, gpre_ref, win_ref, wout_ref, gpost_ref, o_ref):
    o_ref[...] = _ffn_body(h_ref, gpre_ref, win_ref, wout_ref, gpost_ref)


def _ffn_kv_kernel(h_ref, gpre_ref, win_ref, wout_ref, gpost_ref, gkv_ref, wk_ref, wvt_ref, cos_ref, sin_ref,
                   o_ref, k_ref, vt_ref):
    h_new = _ffn_body(h_ref, gpre_ref, win_ref, wout_ref, gpost_ref)
    o_ref[...] = h_new
    hn = _rms(h_new, gkv_ref[...]).astype(BF16)
    k = _dot(hn, wk_ref[...])
    cos, sin = cos_ref[...], sin_ref[...]
    for j in range(KV_WIDTH // LANES):
        k_ref[:, j * LANES:(j + 1) * LANES] = _rope(k[:, j * LANES:(j + 1) * LANES], cos, sin).astype(BF16)
    vt_ref[0] = _dot_nt(wvt_ref[...], hn).astype(BF16)


def _ffn(h2d, gpre, w_in, w_out, gpost, kv_args=None):
    tokens = h2d.shape[0]
    tm = TM_FFN
    row_spec = pl.BlockSpec((tm, D_MODEL), lambda i: (i, 0))
    in_specs = [row_spec, _const_spec((1, D_MODEL)), _const_spec(w_in.shape), _const_spec(w_out.shape),
                _const_spec((1, D_MODEL))]
    args = [h2d, gpre.reshape(1, D_MODEL), w_in, w_out, gpost.reshape(1, D_MODEL)]
    params = pltpu.CompilerParams(dimension_semantics=("arbitrary",), vmem_limit_bytes=VMEM_LIMIT)
    if kv_args is None:
        return pl.pallas_call(
            _ffn_kernel, grid=(tokens // tm,), in_specs=in_specs, out_specs=row_spec,
            out_shape=jax.ShapeDtypeStruct((tokens, D_MODEL), F32),
            compiler_params=params, name="ffn",
        )(*args)
    gkv, wk, wvt, cos, sin = kv_args
    pos_blocks = SEQ // tm
    table_spec = pl.BlockSpec((tm, LANES), lambda i: (i % pos_blocks, 0))
    k_spec = pl.BlockSpec((tm, KV_WIDTH), lambda i: (i, 0))
    vt_spec = pl.BlockSpec((1, KV_WIDTH, tm), lambda i: (i // pos_blocks, 0, i % pos_blocks))
    return pl.pallas_call(
        _ffn_kv_kernel, grid=(tokens // tm,),
        in_specs=in_specs + [_const_spec((1, D_MODEL)), _const_spec(wk.shape), _const_spec(wvt.shape),
                             table_spec, table_spec],
        out_specs=[row_spec, k_spec, vt_spec],
        out_shape=[jax.ShapeDtypeStruct((tokens, D_MODEL), F32),
                   jax.ShapeDtypeStruct((tokens, KV_WIDTH), BF16),
                   jax.ShapeDtypeStruct((BATCH, KV_WIDTH, SEQ), BF16)],
        compiler_params=params, name="ffn_kv",
    )(*args, gkv.reshape(1, D_MODEL), wk, wvt, cos, sin)


def _pad_head_rows(q, half):
    z = jnp.zeros_like(q)
    return jnp.concatenate([z, q] if half else [q, z], axis=0)


def _mixer_b_kernel(sinks_ref, x_ref, gpre_ref, wint_ref, cost_ref, sint_ref, k_ref, vt_ref, mkv_ref, mvt_ref,
                    wout_ref, gpost_ref, o_ref, qt_s, s_s, sm_s, catt_s):
    b = pl.program_id(0)
    j = pl.program_id(1)
    tq = TQ_B
    n_qb = tq // WINDOW
    q_width = SWA_Q_HEADS * HEAD_DIM
    grp_rows = SWA_GROUP * HEAD_DIM
    x = x_ref[0]
    hn = _rms(x, gpre_ref[...]).astype(BF16)

    cos_t, sin_t = cost_ref[...], sint_ref[...]
    for kh in range(SWA_KV_HEADS):
        pt = _dot_nt(wint_ref[kh * grp_rows:(kh + 1) * grp_rows, :], hn)
        for g in range(SWA_GROUP):
            r0 = g * HEAD_DIM
            x1, x2 = pt[r0:r0 + ROPE_HALF], pt[r0 + ROPE_HALF:r0 + ROPE_DIM]
            head = jnp.concatenate([x1 * cos_t - x2 * sin_t, x2 * cos_t + x1 * sin_t,
                                    pt[r0 + ROPE_DIM:r0 + HEAD_DIM]], axis=0)
            qt_s[kh * grp_rows + r0:kh * grp_rows + r0 + HEAD_DIM, :] = (head * ATTN_SCALE).astype(BF16)
    qt_s[q_width:, :] = (_dot_nt(wint_ref[q_width:, :], hn) * ATTN_SCALE).astype(BF16)

    ci = lax.broadcasted_iota(jnp.int32, (2 * WINDOW, SWA_GROUP * WINDOW), 0)
    qi = lax.broadcasted_iota(jnp.int32, (2 * WINDOW, SWA_GROUP * WINDOW), 1) & (WINDOW - 1)
    dist = ci - qi
    band_mask = jnp.logical_and(dist > 0, dist <= WINDOW)
    first_mask = jnp.logical_and(band_mask, jnp.logical_or(ci >= WINDOW, j > 0))

    def starts(qb):
        start = pl.multiple_of(j * tq + qb * WINDOW, WINDOW)
        return pl.multiple_of(jnp.maximum(start - WINDOW, 0), WINDOW), start

    def swa_scores(qb, kh):
        pstart, start = starts(qb)
        slab = slice((kh // 2) * LANES, (kh // 2 + 1) * LANES)
        band = jnp.concatenate([k_ref[0, pl.ds(pstart, WINDOW), slab], k_ref[0, pl.ds(start, WINDOW), slab]],
                               axis=0)
        qs = slice(qb * WINDOW, (qb + 1) * WINDOW)
        qcat = jnp.concatenate([qt_s[(kh * SWA_GROUP + g) * HEAD_DIM:(kh * SWA_GROUP + g + 1) * HEAD_DIM, qs]
                                for g in range(SWA_GROUP)], axis=1)
        s_s[qb % 2, kh] = _dot(band, _pad_head_rows(qcat, kh % 2))

    def swa_output(qb, kh):
        pstart, start = starts(qb)
        mask = first_mask if qb == 0 else band_mask
        s = jnp.where(mask, s_s[qb % 2, kh], NEG_INF)
        sink = jnp.concatenate([jnp.full((1, WINDOW), sinks_ref[kh * SWA_GROUP + g], F32)
                                for g in range(SWA_GROUP)], axis=1)
        m = jnp.maximum(jnp.max(s, axis=0, keepdims=True), sink)
        e = jnp.exp(s - m)
        den = jnp.sum(e, axis=0, keepdims=True) + jnp.exp(sink - m)
        rows = slice(kh * HEAD_DIM, (kh + 1) * HEAD_DIM)
        vband = jnp.concatenate([vt_ref[0, rows, pl.ds(pstart, WINDOW)], vt_ref[0, rows, pl.ds(start, WINDOW)]],
                                axis=1)
        ot = _dot(vband, e.astype(BF16)) * (1.0 / den)
        for g in range(SWA_GROUP):
            h = kh * SWA_GROUP + g
            catt_s[h * HEAD_DIM:(h + 1) * HEAD_DIM, qb * WINDOW:(qb + 1) * WINDOW] = (
                ot[:, g * WINDOW:(g + 1) * WINDOW].astype(BF16))

    def mem_scores(h):
        slab = slice((h // 2) * LANES, (h // 2 + 1) * LANES)
        q = qt_s[q_width + h * HEAD_DIM:q_width + (h + 1) * HEAD_DIM, :]
        sm_s[h] = _dot(mkv_ref[b, :, slab], _pad_head_rows(q, h % 2))

    def mem_output(h):
        s = sm_s[h]
        m = jnp.max(s, axis=0, keepdims=True)
        e = jnp.exp(s - m)
        den = jnp.sum(e, axis=0, keepdims=True)
        ot = _dot(mvt_ref[b, h * HEAD_DIM:(h + 1) * HEAD_DIM, :], e.astype(BF16)) * (1.0 / den)
        catt_s[q_width + h * HEAD_DIM:q_width + (h + 1) * HEAD_DIM, :] = ot.astype(BF16)

    for kh in range(SWA_KV_HEADS):
        swa_scores(0, kh)
    for qb in range(n_qb):
        for kh in range(SWA_KV_HEADS):
            if qb + 1 < n_qb:
                swa_scores(qb + 1, kh)
            else:
                mem_scores(kh)
            swa_output(qb, kh)
    for h in range(MEM_HEADS):
        mem_output(h)

    mixed = lax.dot_general(catt_s[...], wout_ref[...], (((0,), (0,)), ((), ())), preferred_element_type=F32)
    o_ref[0] = x + _rms(mixed, gpost_ref[...])


def _mixer_b(h, sinks, gpre, w_in_t, cos_t, sin_t, k, vt, mkv, mvt, w_out, gpost):
    tq = TQ_B
    grid_spec = pltpu.PrefetchScalarGridSpec(
        num_scalar_prefetch=1,
        grid=(BATCH, SEQ // tq),
        in_specs=[
            pl.BlockSpec((1, tq, D_MODEL), lambda b, j, s: (b, j, 0)),
            pl.BlockSpec((1, D_MODEL), lambda b, j, s: (0, 0)),
            pl.BlockSpec(w_in_t.shape, lambda b, j, s: (0, 0)),
            pl.BlockSpec((ROPE_HALF, tq), lambda b, j, s: (0, j)),
            pl.BlockSpec((ROPE_HALF, tq), lambda b, j, s: (0, j)),
            pl.BlockSpec((1, SEQ, KV_WIDTH), lambda b, j, s: (b, 0, 0)),
            pl.BlockSpec((1, KV_WIDTH, SEQ), lambda b, j, s: (b, 0, 0)),
            pl.BlockSpec((None, BATCH, N_MEM, 2 * MEM_WIDTH), lambda b, j, s: (1, 0, 0, 0)),
            pl.BlockSpec((None, BATCH, MEM_WIDTH, N_MEM), lambda b, j, s: (1, 0, 0, 0)),
            pl.BlockSpec(w_out.shape, lambda b, j, s: (0, 0)),
            pl.BlockSpec((1, D_MODEL), lambda b, j, s: (0, 0)),
        ],
        out_specs=pl.BlockSpec((1, tq, D_MODEL), lambda b, j, s: (b, j, 0)),
        scratch_shapes=[
            pltpu.VMEM((D_MODEL, tq), BF16),
            pltpu.VMEM((2, SWA_KV_HEADS, 2 * WINDOW, SWA_GROUP * WINDOW), F32),
            pltpu.VMEM((MEM_HEADS, N_MEM, tq), F32),
            pltpu.VMEM((D_MODEL, tq), BF16),
        ],
    )
    return pl.pallas_call(
        _mixer_b_kernel,
        grid_spec=grid_spec,
        out_shape=jax.ShapeDtypeStruct((BATCH, SEQ, D_MODEL), F32),
        compiler_params=pltpu.CompilerParams(dimension_semantics=("arbitrary", "arbitrary"),
                                             vmem_limit_bytes=VMEM_LIMIT),
        name="mixer_b",
    )(sinks, h, gpre.reshape(1, D_MODEL), w_in_t, cos_t, sin_t, k, vt, mkv, mvt, w_out,
      gpost.reshape(1, D_MODEL))


def _rope_tables():
    inv_freq = 1.0 / (ROPE_THETA ** (jnp.arange(0, ROPE_DIM, 2, dtype=F32) / ROPE_DIM))
    ang = jnp.arange(SEQ, dtype=F32)[:, None] * inv_freq[None, :]
    c, s = jnp.cos(ang), jnp.sin(ang)
    rest = HEAD_DIM - ROPE_DIM
    cos_h = jnp.concatenate([c, c, jnp.ones((SEQ, rest), F32)], axis=-1)
    sin_h = jnp.concatenate([-s, s, jnp.zeros((SEQ, rest), F32)], axis=-1)
    reps = LANES // HEAD_DIM
    return jnp.tile(cos_h, (1, reps)), jnp.tile(sin_h, (1, reps)), c.T, s.T


def kernel(x, mem, norm_mix_pre, norm_mix_post, norm_ffn_pre, norm_ffn_post, mem_norm, w_mem_kv, w_in_a,
           conv_w, conv_b, w_gate_r, b_gate_r, w_gate_i, b_gate_i, lru_lambda, norm_kv, w_kv_shared, w_in_b,
           sinks, w_out, w_ffn_in, w_ffn_out):
    tokens = BATCH * SEQ
    cos, sin, cos_t, sin_t = _rope_tables()
    w_gate = jnp.concatenate([w_gate_r[0], w_gate_i[0]], axis=-1).astype(BF16)
    b_gate = jnp.concatenate([b_gate_r[0].reshape(LRU_BLOCKS, 1, LRU_BLOCK),
                              b_gate_i[0].reshape(LRU_BLOCKS, 1, LRU_BLOCK)], axis=-1)

    w_mem_vt = jnp.swapaxes(w_mem_kv[:, :, MEM_WIDTH:], 1, 2).astype(BF16)
    mkv, mvt = _memkv(mem, mem_norm, w_mem_kv.astype(BF16), w_mem_vt)

    h = _mixer_a(x, norm_mix_pre[0], w_in_a[0].astype(BF16), conv_w[0], conv_b[0], w_gate, b_gate,
                 lru_lambda[0], mkv, w_out[0].astype(BF16), norm_mix_post[0])
    h2d, k, vt = _ffn(h.reshape(tokens, D_MODEL), norm_ffn_pre[0], w_ffn_in[0].astype(BF16),
                      w_ffn_out[0].astype(BF16), norm_ffn_post[0],
                      kv_args=(norm_kv, w_kv_shared[:, :KV_WIDTH].astype(BF16),
                               w_kv_shared[:, KV_WIDTH:].T.astype(BF16), cos, sin))

    h = _mixer_b(h2d.reshape(BATCH, SEQ, D_MODEL), sinks[0], norm_mix_pre[1], w_in_b[0].T.astype(BF16),
                 cos_t, sin_t, k.reshape(BATCH, SEQ, KV_WIDTH), vt, mkv, mvt,
                 w_out[1].astype(BF16), norm_mix_post[1])
    h2d = _ffn(h.reshape(tokens, D_MODEL), norm_ffn_pre[1], w_ffn_in[1].astype(BF16),
               w_ffn_out[1].astype(BF16), norm_ffn_post[1])
    return h2d.reshape(BATCH, SEQ, D_MODEL)
```

```python
import functools
import math

import jax
import jax.numpy as jnp
from jax import lax
from jax.experimental import pallas as pl
from jax.experimental.pallas import tpu as pltpu

D_MODEL = 1024
BATCH = 8
SEQ = 4096
HEAD_DIM = 64
MEM_HEADS = 4
MEM_WIDTH = MEM_HEADS * HEAD_DIM
N_MEM = 256
LRU_WIDTH = D_MODEL - MEM_WIDTH
LRU_BLOCKS = 6
LRU_BLOCK = LRU_WIDTH // LRU_BLOCKS
CONV_WIDTH = 4
LRU_C = 8.0
SWA_Q_HEADS = 12
SWA_KV_HEADS = 4
SWA_GROUP = SWA_Q_HEADS // SWA_KV_HEADS
KV_WIDTH = SWA_KV_HEADS * HEAD_DIM
WINDOW = 128
ROPE_DIM = HEAD_DIM // 4
ROPE_HALF = ROPE_DIM // 2
ROPE_THETA = 500000.0
D_FF = 2816
EPS = 1e-6
NEG_INF = -1e30
ATTN_SCALE = HEAD_DIM ** -0.5

LANES = 128
SUBLANES = 8
VMEM_LIMIT = 56 * 1024 * 1024

TS_A = 128
PITCH_A = TS_A + SUBLANES
TM_FFN = 512
FF_CHUNK = 256
TQ_B = 512

BF16 = jnp.bfloat16
F32 = jnp.float32


def _rms(x, g):
    ms = jnp.mean(x * x, axis=-1, keepdims=True)
    return x * lax.rsqrt(ms + EPS) * g


def _dot(a, b):
    return jnp.dot(a, b, preferred_element_type=F32)


def _dot_nt(a, b):
    return lax.dot_general(a, b, (((1,), (1,)), ((), ())), preferred_element_type=F32)


def _gelu_tanh(x):
    return 0.5 * x * (1.0 + jnp.tanh(math.sqrt(2.0 / math.pi) * (x + 0.044715 * (x * x * x))))


def _rope(x, cos, sin):
    lane = lax.broadcasted_iota(jnp.int32, x.shape, 1) & (HEAD_DIM - 1)
    partner = jnp.where(lane < ROPE_HALF,
                        pltpu.roll(x, LANES - ROPE_HALF, 1),
                        pltpu.roll(x, ROPE_HALF, 1))
    return x * cos + partner * sin


def _memkv_kernel(mem_ref, g_ref, w_ref, wvt_ref, o_ref, ovt_ref):
    mem_n = _rms(mem_ref[0], g_ref[...]).astype(BF16)
    for l in range(2):
        o_ref[l, 0] = _dot(mem_n, w_ref[l]).astype(BF16)
        ovt_ref[l, 0] = _dot_nt(wvt_ref[l], mem_n).astype(BF16)


def _memkv(mem, mem_norm, w_mem_kv, w_mem_vt):
    return pl.pallas_call(
        _memkv_kernel,
        grid=(BATCH,),
        in_specs=[
            pl.BlockSpec((1, N_MEM, D_MODEL), lambda b: (b, 0, 0)),
            pl.BlockSpec((1, D_MODEL), lambda b: (0, 0)),
            pl.BlockSpec((2, D_MODEL, 2 * MEM_WIDTH), lambda b: (0, 0, 0)),
            pl.BlockSpec((2, MEM_WIDTH, D_MODEL), lambda b: (0, 0, 0)),
        ],
        out_specs=[pl.BlockSpec((2, 1, N_MEM, 2 * MEM_WIDTH), lambda b: (0, b, 0, 0)),
                   pl.BlockSpec((2, 1, MEM_WIDTH, N_MEM), lambda b: (0, b, 0, 0))],
        out_shape=[jax.ShapeDtypeStruct((2, BATCH, N_MEM, 2 * MEM_WIDTH), BF16),
                   jax.ShapeDtypeStruct((2, BATCH, MEM_WIDTH, N_MEM), BF16)],
        compiler_params=pltpu.CompilerParams(dimension_semantics=("arbitrary",),
                                             vmem_limit_bytes=VMEM_LIMIT),
        name="memkv",
    )(mem, mem_norm.reshape(1, D_MODEL), w_mem_kv, w_mem_vt)


def _const_spec(shape):
    nd = len(shape)
    return pl.BlockSpec(shape, lambda *_: (0,) * nd, pipeline_mode=pl.Buffered(1))


def _pad_head_rows(q, half):
    z = jnp.zeros_like(q)
    return jnp.concatenate([z, q] if half else [q, z], axis=0)


def _mixer_a_kernel(x_ref, gpre_ref, win_ref, wqt_ref, convw_ref, convb_ref, wg_ref, bg_ref, lam_ref,
                    mkv_ref, mvt_ref, wout_ref, gpost_ref, o_ref,
                    xr_s, xc_s, a_s, u_s, qt_s, sm_s, mt_s, cat_s, h_s):
    i = pl.program_id(0)
    ts, pitch = TS_A, PITCH_A
    rows = BATCH * ts

    @pl.when(i == 0)
    def _():
        h_s[...] = jnp.zeros_like(h_s)
        xr_s[:, 0:SUBLANES, :] = jnp.zeros((BATCH, SUBLANES, LRU_WIDTH), F32)

    x = x_ref[...].reshape(rows, D_MODEL)
    hn = _rms(x, gpre_ref[...]).astype(BF16)

    xr = _dot(hn, win_ref[:, 0:LRU_WIDTH])
    for b in range(BATCH):
        xr_s[b, SUBLANES:SUBLANES + ts, :] = xr[b * ts:(b + 1) * ts]
    for b in range(BATCH):
        acc = jnp.broadcast_to(convb_ref[...], (ts, LRU_WIDTH))
        for k in range(CONV_WIDTH):
            off = SUBLANES - (CONV_WIDTH - 1) + k
            acc = acc + xr_s[b, off:off + ts, :] * convw_ref[k:k + 1, :]
        xc_s[b * ts:(b + 1) * ts, :] = acc
        xr_s[b, 0:SUBLANES, :] = xr_s[b, ts:ts + SUBLANES, :]

    lam = lam_ref[...]
    log_sig = jnp.minimum(lam, 0.0) - jnp.log1p(jnp.exp(-jnp.abs(lam)))
    ls_c = LRU_C * log_sig
    tpos = lax.broadcasted_iota(jnp.int32, (rows, 1), 0) & (ts - 1)
    seq_start = jnp.logical_and(tpos == 0, i == 0)
    for c in range(LRU_BLOCKS):
        cs = slice(c * LRU_BLOCK, (c + 1) * LRU_BLOCK)
        xc = xc_s[:, cs]
        g = _dot(xc.astype(BF16), wg_ref[c]) + bg_ref[c]
        r = jax.nn.sigmoid(g[:, :LRU_BLOCK])
        ig = jax.nn.sigmoid(g[:, LRU_BLOCK:])
        a = jnp.exp(r * ls_c[:, cs])
        mult = jnp.sqrt(jnp.maximum(1.0 - a * a, 0.0))
        mult = jnp.where(seq_start, 1.0, mult)
        u = mult * (ig * xc)
        for b in range(BATCH):
            a_s[c, b * pitch:b * pitch + ts, :] = a[b * ts:(b + 1) * ts]
            u_s[c, b * pitch:b * pitch + ts, :] = u[b * ts:(b + 1) * ts]

        if c % 2 == 1:
            gs = slice((c - 1) * LRU_BLOCK, (c + 1) * LRU_BLOCK)
            xc_s[:, gs] = _gelu_tanh(_dot(hn, win_ref[:, LRU_WIDTH + gs.start:LRU_WIDTH + gs.stop]))

    qt_s[...] = (_dot_nt(wqt_ref[...], hn) * ATTN_SCALE).astype(BF16)

    def mem_scores(b, h):
        slab = slice((h // 2) * LANES, (h // 2 + 1) * LANES)
        q = qt_s[h * HEAD_DIM:(h + 1) * HEAD_DIM, b * ts:(b + 1) * ts]
        sm_s[b % 2, h] = _dot(mkv_ref[b, :, slab], _pad_head_rows(q, h % 2))

    def mem_output(b, h):
        s = sm_s[b % 2, h]
        m = jnp.max(s, axis=0, keepdims=True)
        e = jnp.exp(s - m)
        den = jnp.sum(e, axis=0, keepdims=True)
        ot = _dot(mvt_ref[b, h * HEAD_DIM:(h + 1) * HEAD_DIM, :], e.astype(BF16)) * (1.0 / den)
        mt_s[h * HEAD_DIM:(h + 1) * HEAD_DIM, b * ts:(b + 1) * ts] = ot

    h = [h_s[:, c * LRU_BLOCK:(c + 1) * LRU_BLOCK] for c in range(LRU_BLOCKS)]
    steps_per_chunk = ts // BATCH
    for hd in range(MEM_HEADS):
        mem_scores(0, hd)
    for b in range(BATCH):
        for t in range(b * steps_per_chunk, (b + 1) * steps_per_chunk):
            idx = pl.ds(t, BATCH, stride=pitch)
            for c in range(LRU_BLOCKS):
                h[c] = a_s[c, idx, :] * h[c] + u_s[c, idx, :]
                u_s[c, idx, :] = h[c]
        for hd in range(MEM_HEADS):
            if b + 1 < BATCH:
                mem_scores(b + 1, hd)
            mem_output(b, hd)
    for c in range(LRU_BLOCKS):
        h_s[:, c * LRU_BLOCK:(c + 1) * LRU_BLOCK] = h[c]

    for b in range(BATCH):
        rs = slice(b * ts, (b + 1) * ts)
        for c in range(LRU_BLOCKS):
            cs = slice(c * LRU_BLOCK, (c + 1) * LRU_BLOCK)
            cat_s[rs, cs] = (u_s[c, b * pitch:b * pitch + ts, :] * xc_s[rs, cs]).astype(BF16)
        cat_s[rs, LRU_WIDTH:] = mt_s[:, rs].T.astype(BF16)

    mixed = _dot(cat_s[...], wout_ref[...])
    o_ref[...] = (x + _rms(mixed, gpost_ref[...])).reshape(BATCH, ts, D_MODEL)


def _mixer_a(x, gpre, w_in, w_q_t, conv_w, conv_b, w_gate, b_gate, lam, mkv, mvt, w_out, gpost):
    ts, pitch = TS_A, PITCH_A
    rows = BATCH * ts
    return pl.pallas_call(
        _mixer_a_kernel,
        grid=(SEQ // ts,),
        in_specs=[
            pl.BlockSpec((BATCH, ts, D_MODEL), lambda i: (0, i, 0)),
            _const_spec((1, D_MODEL)),
            _const_spec(w_in.shape),
            _const_spec(w_q_t.shape),
            _const_spec(conv_w.shape),
            _const_spec((1, LRU_WIDTH)),
            _const_spec(w_gate.shape),
            _const_spec(b_gate.shape),
            _const_spec((1, LRU_WIDTH)),
            pl.BlockSpec((None, BATCH, N_MEM, 2 * MEM_WIDTH), lambda i: (0, 0, 0, 0), pipeline_mode=pl.Buffered(1)),
            pl.BlockSpec((None, BATCH, MEM_WIDTH, N_MEM), lambda i: (0, 0, 0, 0), pipeline_mode=pl.Buffered(1)),
            _const_spec(w_out.shape),
            _const_spec((1, D_MODEL)),
        ],
        out_specs=pl.BlockSpec((BATCH, ts, D_MODEL), lambda i: (0, i, 0)),
        out_shape=jax.ShapeDtypeStruct((BATCH, SEQ, D_MODEL), F32),
        scratch_shapes=[
            pltpu.VMEM((BATCH, ts + SUBLANES, LRU_WIDTH), F32),
            pltpu.VMEM((rows, LRU_WIDTH), F32),
            pltpu.VMEM((LRU_BLOCKS, BATCH * pitch, LRU_BLOCK), F32),
            pltpu.VMEM((LRU_BLOCKS, BATCH * pitch, LRU_BLOCK), F32),
            pltpu.VMEM((MEM_WIDTH, rows), BF16),
            pltpu.VMEM((2, MEM_HEADS, N_MEM, ts), F32),
            pltpu.VMEM((MEM_WIDTH, rows), F32),
            pltpu.VMEM((rows, D_MODEL), BF16),
            pltpu.VMEM((BATCH, LRU_WIDTH), F32),
        ],
        compiler_params=pltpu.CompilerParams(dimension_semantics=("arbitrary",),
                                             vmem_limit_bytes=VMEM_LIMIT),
        name="mixer_a",
    )(x, gpre.reshape(1, D_MODEL), w_in, w_q_t, conv_w, conv_b.reshape(1, LRU_WIDTH), w_gate, b_gate,
      lam.reshape(1, LRU_WIDTH), mkv, mvt, w_out, gpost.reshape(1, D_MODEL))


def _ffn_body(h_ref, gpre_ref, win_ref, wout_ref, gpost_ref):
    h = h_ref[...]
    hn = _rms(h, gpre_ref[...]).astype(BF16)
    acc = jnp.zeros((h.shape[0], D_MODEL), F32)
    for ck in range(D_FF // FF_CHUNK):
        lo = ck * FF_CHUNK
        g = _dot(hn, win_ref[:, lo:lo + FF_CHUNK])
        u = _dot(hn, win_ref[:, D_FF + lo:D_FF + lo + FF_CHUNK])
        act = (g * jax.nn.sigmoid(g) * u).astype(BF16)
        acc = acc + _dot(act, wout_ref[lo:lo + FF_CHUNK, :])
    return h + _rms(acc, gpost_ref[...])


def _ffn_kernel(h_ref, gpre_ref, win_ref, wout_ref, gpost_ref, o_ref):
    o_ref[...] = _ffn_body(h_ref, gpre_ref, win_ref, wout_ref, gpost_ref)


def _ffn_kv_kernel(h_ref, gpre_ref, win_ref, wout_ref, gpost_ref, gkv_ref, wk_ref, wvt_ref, cos_ref, sin_ref,
                   o_ref, k_ref, vt_ref):
    h_new = _ffn_body(h_ref, gpre_ref, win_ref, wout_ref, gpost_ref)
    o_ref[...] = h_new
    hn = _rms(h_new, gkv_ref[...]).astype(BF16)
    k = _dot(hn, wk_ref[...])
    cos, sin = cos_ref[...], sin_ref[...]
    for j in range(KV_WIDTH // LANES):
        k_ref[:, j * LANES:(j + 1) * LANES] = _rope(k[:, j * LANES:(j + 1) * LANES], cos, sin).astype(BF16)
    vt_ref[0] = _dot_nt(wvt_ref[...], hn).astype(BF16)


def _ffn(h2d, gpre, w_in, w_out, gpost, kv_args=None):
    tokens = h2d.shape[0]
    tm = TM_FFN
    row_spec = pl.BlockSpec((tm, D_MODEL), lambda i: (i, 0))
    in_specs = [row_spec, _const_spec((1, D_MODEL)), _const_spec(w_in.shape), _const_spec(w_out.shape),
                _const_spec((1, D_MODEL))]
    args = [h2d, gpre.reshape(1, D_MODEL), w_in, w_out, gpost.reshape(1, D_MODEL)]
    params = pltpu.CompilerParams(dimension_semantics=("arbitrary",), vmem_limit_bytes=VMEM_LIMIT)
    if kv_args is None:
        return pl.pallas_call(
            _ffn_kernel, grid=(tokens // tm,), in_specs=in_specs, out_specs=row_spec,
            out_shape=jax.ShapeDtypeStruct((tokens, D_MODEL), F32),
            compiler_params=params, name="ffn",
        )(*args)
    gkv, wk, wvt, cos, sin = kv_args
    pos_blocks = SEQ // tm
    table_spec = pl.BlockSpec((tm, LANES), lambda i: (i % pos_blocks, 0))
    k_spec = pl.BlockSpec((tm, KV_WIDTH), lambda i: (i, 0))
    vt_spec = pl.BlockSpec((1, KV_WIDTH, tm), lambda i: (i // pos_blocks, 0, i % pos_blocks))
    return pl.pallas_call(
        _ffn_kv_kernel, grid=(tokens // tm,),
        in_specs=in_specs + [_const_spec((1, D_MODEL)), _const_spec(wk.shape), _const_spec(wvt.shape),
                             table_spec, table_spec],
        out_specs=[row_spec, k_spec, vt_spec],
        out_shape=[jax.ShapeDtypeStruct((tokens, D_MODEL), F32),
                   jax.ShapeDtypeStruct((tokens, KV_WIDTH), BF16),
                   jax.ShapeDtypeStruct((BATCH, KV_WIDTH, SEQ), BF16)],
        compiler_params=params, name="ffn_kv",
    )(*args, gkv.reshape(1, D_MODEL), wk, wvt, cos, sin)


def _mixer_b_kernel(sinks_ref, x_ref, gpre_ref, wint_ref, cost_ref, sint_ref, k_ref, vt_ref, mkv_ref, mvt_ref,
                    wout_ref, gpost_ref, o_ref, qt_s, s_s, sm_s, catt_s):
    b = pl.program_id(0)
    j = pl.program_id(1)
    tq = TQ_B
    n_qb = tq // WINDOW
    q_width = SWA_Q_HEADS * HEAD_DIM
    grp_rows = SWA_GROUP * HEAD_DIM
    x = x_ref[0]
    hn = _rms(x, gpre_ref[...]).astype(BF16)

    cos_t, sin_t = cost_ref[...], sint_ref[...]
    for kh in range(SWA_KV_HEADS):
        pt = _dot_nt(wint_ref[kh * grp_rows:(kh + 1) * grp_rows, :], hn)
        for g in range(SWA_GROUP):
            r0 = g * HEAD_DIM
            x1, x2 = pt[r0:r0 + ROPE_HALF], pt[r0 + ROPE_HALF:r0 + ROPE_DIM]
            head = jnp.concatenate([x1 * cos_t - x2 * sin_t, x2 * cos_t + x1 * sin_t,
                                    pt[r0 + ROPE_DIM:r0 + HEAD_DIM]], axis=0)
            qt_s[kh * grp_rows + r0:kh * grp_rows + r0 + HEAD_DIM, :] = (head * ATTN_SCALE).astype(BF16)
    qt_s[q_width:, :] = (_dot_nt(wint_ref[q_width:, :], hn) * ATTN_SCALE).astype(BF16)

    ci = lax.broadcasted_iota(jnp.int32, (2 * WINDOW, SWA_GROUP * WINDOW), 0)
    qi = lax.broadcasted_iota(jnp.int32, (2 * WINDOW, SWA_GROUP * WINDOW), 1) & (WINDOW - 1)
    dist = ci - qi
    band_mask = jnp.logical_and(dist > 0, dist <= WINDOW)
    first_mask = jnp.logical_and(band_mask, jnp.logical_or(ci >= WINDOW, j > 0))

    def starts(qb):
        start = pl.multiple_of(j * tq + qb * WINDOW, WINDOW)
        return pl.multiple_of(jnp.maximum(start - WINDOW, 0), WINDOW), start

    def swa_scores(qb, kh):
        pstart, start = starts(qb)
        slab = slice((kh // 2) * LANES, (kh // 2 + 1) * LANES)
        band = jnp.concatenate([k_ref[0, pl.ds(pstart, WINDOW), slab], k_ref[0, pl.ds(start, WINDOW), slab]],
                               axis=0)
        qs = slice(qb * WINDOW, (qb + 1) * WINDOW)
        qcat = jnp.concatenate([qt_s[(kh * SWA_GROUP + g) * HEAD_DIM:(kh * SWA_GROUP + g + 1) * HEAD_DIM, qs]
                                for g in range(SWA_GROUP)], axis=1)
        s_s[qb % 2, kh] = _dot(band, _pad_head_rows(qcat, kh % 2))

    def swa_output(qb, kh):
        pstart, start = starts(qb)
        mask = first_mask if qb == 0 else band_mask
        s = jnp.where(mask, s_s[qb % 2, kh], NEG_INF)
        sink = jnp.concatenate([jnp.full((1, WINDOW), sinks_ref[kh * SWA_GROUP + g], F32)
                                for g in range(SWA_GROUP)], axis=1)
        m = jnp.maximum(jnp.max(s, axis=0, keepdims=True), sink)
        e = jnp.exp(s - m)
        den = jnp.sum(e, axis=0, keepdims=True) + jnp.exp(sink - m)
        rows = slice(kh * HEAD_DIM, (kh + 1) * HEAD_DIM)
        vband = jnp.concatenate([vt_ref[0, rows, pl.ds(pstart, WINDOW)], vt_ref[0, rows, pl.ds(start, WINDOW)]],
                                axis=1)
        ot = _dot(vband, e.astype(BF16)) * (1.0 / den)
        for g in range(SWA_GROUP):
            h = kh * SWA_GROUP + g
            catt_s[h * HEAD_DIM:(h + 1) * HEAD_DIM, qb * WINDOW:(qb + 1) * WINDOW] = (
                ot[:, g * WINDOW:(g + 1) * WINDOW].astype(BF16))

    def mem_scores(h):
        slab = slice((h // 2) * LANES, (h // 2 + 1) * LANES)
        q = qt_s[q_width + h * HEAD_DIM:q_width + (h + 1) * HEAD_DIM, :]
        sm_s[h] = _dot(mkv_ref[b, :, slab], _pad_head_rows(q, h % 2))

    def mem_output(h):
        s = sm_s[h]
        m = jnp.max(s, axis=0, keepdims=True)
        e = jnp.exp(s - m)
        den = jnp.sum(e, axis=0, keepdims=True)
        ot = _dot(mvt_ref[b, h * HEAD_DIM:(h + 1) * HEAD_DIM, :], e.astype(BF16)) * (1.0 / den)
        catt_s[q_width + h * HEAD_DIM:q_width + (h + 1) * HEAD_DIM, :] = ot.astype(BF16)

    for kh in range(SWA_KV_HEADS):
        swa_scores(0, kh)
    for qb in range(n_qb):
        for kh in range(SWA_KV_HEADS):
            if qb + 1 < n_qb:
                swa_scores(qb + 1, kh)
            else:
                mem_scores(kh)
            swa_output(qb, kh)
    for h in range(MEM_HEADS):
        mem_output(h)

    mixed = lax.dot_general(catt_s[...], wout_ref[...], (((0,), (0,)), ((), ())), preferred_element_type=F32)
    o_ref[0] = x + _rms(mixed, gpost_ref[...])


def _mixer_b(h, sinks, gpre, w_in_t, cos_t, sin_t, k, vt, mkv, mvt, w_out, gpost):
    tq = TQ_B
    grid_spec = pltpu.PrefetchScalarGridSpec(
        num_scalar_prefetch=1,
        grid=(BATCH, SEQ // tq),
        in_specs=[
            pl.BlockSpec((1, tq, D_MODEL), lambda b, j, s: (b, j, 0)),
            _const_spec((1, D_MODEL)),
            _const_spec(w_in_t.shape),
            pl.BlockSpec((ROPE_HALF, tq), lambda b, j, s: (0, j)),
            pl.BlockSpec((ROPE_HALF, tq), lambda b, j, s: (0, j)),
            pl.BlockSpec((1, SEQ, KV_WIDTH), lambda b, j, s: (b, 0, 0)),
            pl.BlockSpec((1, KV_WIDTH, SEQ), lambda b, j, s: (b, 0, 0)),
            pl.BlockSpec((None, BATCH, N_MEM, 2 * MEM_WIDTH), lambda b, j, s: (1, 0, 0, 0),
                         pipeline_mode=pl.Buffered(1)),
            pl.BlockSpec((None, BATCH, MEM_WIDTH, N_MEM), lambda b, j, s: (1, 0, 0, 0),
                         pipeline_mode=pl.Buffered(1)),
            _const_spec(w_out.shape),
            _const_spec((1, D_MODEL)),
        ],
        out_specs=pl.BlockSpec((1, tq, D_MODEL), lambda b, j, s: (b, j, 0)),
        scratch_shapes=[
            pltpu.VMEM((D_MODEL, tq), BF16),
            pltpu.VMEM((2, SWA_KV_HEADS, 2 * WINDOW, SWA_GROUP * WINDOW), F32),
            pltpu.VMEM((MEM_HEADS, N_MEM, tq), F32),
            pltpu.VMEM((D_MODEL, tq), BF16),
        ],
    )
    return pl.pallas_call(
        _mixer_b_kernel,
        grid_spec=grid_spec,
        out_shape=jax.ShapeDtypeStruct((BATCH, SEQ, D_MODEL), F32),
        compiler_params=pltpu.CompilerParams(dimension_semantics=("arbitrary", "arbitrary"),
                                             vmem_limit_bytes=VMEM_LIMIT),
        name="mixer_b",
    )(sinks, h, gpre.reshape(1, D_MODEL), w_in_t, cos_t, sin_t, k, vt, mkv, mvt, w_out,
      gpost.reshape(1, D_MODEL))


def _rope_tables():
    inv_freq = 1.0 / (ROPE_THETA ** (jnp.arange(0, ROPE_DIM, 2, dtype=F32) / ROPE_DIM))
    ang = jnp.arange(SEQ, dtype=F32)[:, None] * inv_freq[None, :]
    c, s = jnp.cos(ang), jnp.sin(ang)
    rest = HEAD_DIM - ROPE_DIM
    cos_h = jnp.concatenate([c, c, jnp.ones((SEQ, rest), F32)], axis=-1)
    sin_h = jnp.concatenate([-s, s, jnp.zeros((SEQ, rest), F32)], axis=-1)
    reps = LANES // HEAD_DIM
    return jnp.tile(cos_h, (1, reps)), jnp.tile(sin_h, (1, reps)), c.T, s.T


def kernel(x, mem, norm_mix_pre, norm_mix_post, norm_ffn_pre, norm_ffn_post, mem_norm, w_mem_kv, w_in_a,
           conv_w, conv_b, w_gate_r, b_gate_r, w_gate_i, b_gate_i, lru_lambda, norm_kv, w_kv_shared, w_in_b,
           sinks, w_out, w_ffn_in, w_ffn_out):
    tokens = BATCH * SEQ
    cos, sin, cos_t, sin_t = _rope_tables()
    w_gate = jnp.concatenate([w_gate_r[0], w_gate_i[0]], axis=-1).astype(BF16)
    b_gate = jnp.concatenate([b_gate_r[0].reshape(LRU_BLOCKS, 1, LRU_BLOCK),
                              b_gate_i[0].reshape(LRU_BLOCKS, 1, LRU_BLOCK)], axis=-1)

    w_mem_vt = jnp.swapaxes(w_mem_kv[:, :, MEM_WIDTH:], 1, 2).astype(BF16)
    mkv, mvt = _memkv(mem, mem_norm, w_mem_kv.astype(BF16), w_mem_vt)

    h = _mixer_a(x, norm_mix_pre[0], w_in_a[0, :, :2 * LRU_WIDTH].astype(BF16),
                 w_in_a[0, :, 2 * LRU_WIDTH:].T.astype(BF16), conv_w[0], conv_b[0], w_gate, b_gate,
                 lru_lambda[0], mkv, mvt, w_out[0].astype(BF16), norm_mix_post[0])
    h2d, k, vt = _ffn(h.reshape(tokens, D_MODEL), norm_ffn_pre[0], w_ffn_in[0].astype(BF16),
                      w_ffn_out[0].astype(BF16), norm_ffn_post[0],
                      kv_args=(norm_kv, w_kv_shared[:, :KV_WIDTH].astype(BF16),
                               w_kv_shared[:, KV_WIDTH:].T.astype(BF16), cos, sin))

    h = _mixer_b(h2d.reshape(BATCH, SEQ, D_MODEL), sinks[0], norm_mix_pre[1], w_in_b[0].T.astype(BF16),
                 cos_t, sin_t, k.reshape(BATCH, SEQ, KV_WIDTH), vt, mkv, mvt,
                 w_out[1].astype(BF16), norm_mix_post[1])
    h2d = _ffn(h.reshape(tokens, D_MODEL), norm_ffn_pre[1], w_ffn_in[1].astype(BF16),
               w_ffn_out[1].astype(BF16), norm_ffn_post[1])
    return h2d.reshape(BATCH, SEQ, D_MODEL)
```

```python
import functools
import math

import jax
import jax.numpy as jnp
from jax import lax
from jax.experimental import pallas as pl
from jax.experimental.pallas import tpu as pltpu

D_MODEL = 1024
BATCH = 8
SEQ = 4096
HEAD_DIM = 64
MEM_HEADS = 4
MEM_WIDTH = MEM_HEADS * HEAD_DIM
N_MEM = 256
LRU_WIDTH = D_MODEL - MEM_WIDTH
LRU_BLOCKS = 6
LRU_BLOCK = LRU_WIDTH // LRU_BLOCKS
CONV_WIDTH = 4
LRU_C = 8.0
SWA_Q_HEADS = 12
SWA_KV_HEADS = 4
SWA_GROUP = SWA_Q_HEADS // SWA_KV_HEADS
KV_WIDTH = SWA_KV_HEADS * HEAD_DIM
WINDOW = 128
ROPE_DIM = HEAD_DIM // 4
ROPE_HALF = ROPE_DIM // 2
ROPE_THETA = 500000.0
D_FF = 2816
EPS = 1e-6
NEG_INF = -1e30
ATTN_SCALE = HEAD_DIM ** -0.5

LANES = 128
SUBLANES = 8
VMEM_LIMIT = 56 * 1024 * 1024

TS_A = 128
PITCH_A = TS_A + SUBLANES
TM_FFN = 512
FF_CHUNK = 256
TQ_B = 512
PROJ_CHUNKS_B = 2
OUT_CHUNKS_A = 4

BF16 = jnp.bfloat16
F32 = jnp.float32


def _rms(x, g):
    ms = jnp.mean(x * x, axis=-1, keepdims=True)
    return x * lax.rsqrt(ms + EPS) * g


def _dot(a, b):
    return jnp.dot(a, b, preferred_element_type=F32)


def _dot_nt(a, b):
    return lax.dot_general(a, b, (((1,), (1,)), ((), ())), preferred_element_type=F32)


def _gelu_tanh(x):
    c = math.sqrt(2.0 / math.pi)
    hx = 0.5 * x
    return hx + hx * jnp.tanh(x * (c + (c * 0.044715) * (x * x)))


def _rope(x, cos, sin):
    lane = lax.broadcasted_iota(jnp.int32, x.shape, 1) & (HEAD_DIM - 1)
    partner = jnp.where(lane < ROPE_HALF,
                        pltpu.roll(x, LANES - ROPE_HALF, 1),
                        pltpu.roll(x, ROPE_HALF, 1))
    return x * cos + partner * sin


def _memkv_kernel(mem_ref, g_ref, w_ref, wvt_ref, o_ref, ovt_ref):
    mem_n = _rms(mem_ref[0], g_ref[...]).astype(BF16)
    for l in range(2):
        o_ref[l, 0] = _dot(mem_n, w_ref[l]).astype(BF16)
        ovt_ref[l, 0] = _dot_nt(wvt_ref[l], mem_n).astype(BF16)


def _memkv(mem, mem_norm, w_mem_kv, w_mem_vt):
    return pl.pallas_call(
        _memkv_kernel,
        grid=(BATCH,),
        in_specs=[
            pl.BlockSpec((1, N_MEM, D_MODEL), lambda b: (b, 0, 0)),
            pl.BlockSpec((1, D_MODEL), lambda b: (0, 0)),
            pl.BlockSpec((2, D_MODEL, 2 * MEM_WIDTH), lambda b: (0, 0, 0)),
            pl.BlockSpec((2, MEM_WIDTH, D_MODEL), lambda b: (0, 0, 0)),
        ],
        out_specs=[pl.BlockSpec((2, 1, N_MEM, 2 * MEM_WIDTH), lambda b: (0, b, 0, 0)),
                   pl.BlockSpec((2, 1, MEM_WIDTH, N_MEM), lambda b: (0, b, 0, 0))],
        out_shape=[jax.ShapeDtypeStruct((2, BATCH, N_MEM, 2 * MEM_WIDTH), BF16),
                   jax.ShapeDtypeStruct((2, BATCH, MEM_WIDTH, N_MEM), BF16)],
        compiler_params=pltpu.CompilerParams(dimension_semantics=("arbitrary",),
                                             vmem_limit_bytes=VMEM_LIMIT),
        name="memkv",
    )(mem, mem_norm.reshape(1, D_MODEL), w_mem_kv, w_mem_vt)


def _const_spec(shape):
    nd = len(shape)
    return pl.BlockSpec(shape, lambda *_: (0,) * nd, pipeline_mode=pl.Buffered(1))


def _layer_spec(stacked_shape, layer):
    nd = len(stacked_shape) - 1
    return pl.BlockSpec((None,) + tuple(stacked_shape[1:]), lambda *_: (layer,) + (0,) * nd,
                        pipeline_mode=pl.Buffered(1))


def _pad_head_rows(q, half):
    z = jnp.zeros_like(q)
    return jnp.concatenate([z, q] if half else [q, z], axis=0)


def _mixer_a_kernel(x_ref, gpre_ref, win_ref, wqt_ref, convw_ref, convb_ref, wg_ref, bg_ref, lam_ref,
                    mkv_ref, mvt_ref, wout_ref, gpost_ref, o_ref,
                    xr_s, xc_s, a_s, u_s, qt_s, sm_s, mt_s, cat_s, h_s):
    i = pl.program_id(0)
    ts, pitch = TS_A, PITCH_A
    rows = BATCH * ts

    @pl.when(i == 0)
    def _():
        h_s[...] = jnp.zeros_like(h_s)
        xr_s[:, 0:SUBLANES, :] = jnp.zeros((BATCH, SUBLANES, LRU_WIDTH), F32)

    x = x_ref[...].reshape(rows, D_MODEL)
    hn = _rms(x, gpre_ref[...]).astype(BF16)

    xr = _dot(hn, win_ref[:, 0:LRU_WIDTH])
    for b in range(BATCH):
        xr_s[b, SUBLANES:SUBLANES + ts, :] = xr[b * ts:(b + 1) * ts]
    for b in range(BATCH):
        acc = jnp.broadcast_to(convb_ref[...], (ts, LRU_WIDTH))
        for k in range(CONV_WIDTH):
            off = SUBLANES - (CONV_WIDTH - 1) + k
            acc = acc + xr_s[b, off:off + ts, :] * convw_ref[k:k + 1, :]
        xc_s[b * ts:(b + 1) * ts, :] = acc
        xr_s[b, 0:SUBLANES, :] = xr_s[b, ts:ts + SUBLANES, :]

    lam = lam_ref[...]
    log_sig = jnp.minimum(lam, 0.0) - jnp.log1p(jnp.exp(-jnp.abs(lam)))
    ls_c = LRU_C * log_sig
    tpos = lax.broadcasted_iota(jnp.int32, (rows, 1), 0) & (ts - 1)
    seq_start = jnp.logical_and(tpos == 0, i == 0)
    for c in range(LRU_BLOCKS):
        cs = slice(c * LRU_BLOCK, (c + 1) * LRU_BLOCK)
        xc = xc_s[:, cs]
        g = _dot(xc.astype(BF16), wg_ref[c]) + bg_ref[c]
        r = jax.nn.sigmoid(g[:, :LRU_BLOCK])
        ig = jax.nn.sigmoid(g[:, LRU_BLOCK:])
        a = jnp.exp(r * ls_c[:, cs])
        t = 1.0 - a * a
        mult = jnp.where(t > 0.0, t * lax.rsqrt(t), 0.0)
        mult = jnp.where(seq_start, 1.0, mult)
        u = mult * (ig * xc)
        for b in range(BATCH):
            a_s[c, b * pitch:b * pitch + ts, :] = a[b * ts:(b + 1) * ts]
            u_s[c, b * pitch:b * pitch + ts, :] = u[b * ts:(b + 1) * ts]

        if c % 2 == 1:
            gs = slice((c - 1) * LRU_BLOCK, (c + 1) * LRU_BLOCK)
            xc_s[:, gs] = _gelu_tanh(_dot(hn, win_ref[:, LRU_WIDTH + gs.start:LRU_WIDTH + gs.stop]))

    qt_s[...] = (_dot_nt(wqt_ref[...], hn) * ATTN_SCALE).astype(BF16)

    def mem_scores(b, h):
        slab = slice((h // 2) * LANES, (h // 2 + 1) * LANES)
        q = qt_s[h * HEAD_DIM:(h + 1) * HEAD_DIM, b * ts:(b + 1) * ts]
        sm_s[b % 2, h] = _dot(mkv_ref[b, :, slab], _pad_head_rows(q, h % 2))

    def mem_output(b, h):
        s = sm_s[b % 2, h]
        m = jnp.max(s, axis=0, keepdims=True)
        e = jnp.exp(s - m)
        den = jnp.sum(e, axis=0, keepdims=True)
        ot = _dot(mvt_ref[b, h * HEAD_DIM:(h + 1) * HEAD_DIM, :], e.astype(BF16)) * (1.0 / den)
        mt_s[h * HEAD_DIM:(h + 1) * HEAD_DIM, b * ts:(b + 1) * ts] = ot

    h = [h_s[:, c * LRU_BLOCK:(c + 1) * LRU_BLOCK] for c in range(LRU_BLOCKS)]
    steps_per_chunk = ts // BATCH
    for hd in range(MEM_HEADS):
        mem_scores(0, hd)
    for b in range(BATCH):
        for t in range(b * steps_per_chunk, (b + 1) * steps_per_chunk):
            idx = pl.ds(t, BATCH, stride=pitch)
            for c in range(LRU_BLOCKS):
                h[c] = a_s[c, idx, :] * h[c] + u_s[c, idx, :]
                u_s[c, idx, :] = h[c]
        for hd in range(MEM_HEADS):
            if b + 1 < BATCH:
                mem_scores(b + 1, hd)
            mem_output(b, hd)
    for c in range(LRU_BLOCKS):
        h_s[:, c * LRU_BLOCK:(c + 1) * LRU_BLOCK] = h[c]

    nb = BATCH // OUT_CHUNKS_A
    for oc in range(OUT_CHUNKS_A):
        for b in range(oc * nb, (oc + 1) * nb):
            rs = slice(b * ts, (b + 1) * ts)
            for c in range(LRU_BLOCKS):
                cs = slice(c * LRU_BLOCK, (c + 1) * LRU_BLOCK)
                cat_s[rs, cs] = (u_s[c, b * pitch:b * pitch + ts, :] * xc_s[rs, cs]).astype(BF16)
            cat_s[rs, LRU_WIDTH:] = mt_s[:, rs].T.astype(BF16)
        rows_c = slice(oc * nb * ts, (oc + 1) * nb * ts)
        mixed = _dot(cat_s[rows_c, :], wout_ref[...])
        x_c = x_ref[oc * nb:(oc + 1) * nb].reshape(nb * ts, D_MODEL)
        o_ref[oc * nb:(oc + 1) * nb] = (x_c + _rms(mixed, gpost_ref[...])).reshape(nb, ts, D_MODEL)


def _mixer_a(x, gpre, w_in, w_q_t, conv_w, conv_b, w_gate, b_gate, lam, mkv, mvt, w_out_all, gpost):
    ts, pitch = TS_A, PITCH_A
    rows = BATCH * ts
    return pl.pallas_call(
        _mixer_a_kernel,
        grid=(SEQ // ts,),
        in_specs=[
            pl.BlockSpec((BATCH, ts, D_MODEL), lambda i: (0, i, 0)),
            _const_spec((1, D_MODEL)),
            _const_spec(w_in.shape),
            _const_spec(w_q_t.shape),
            _const_spec(conv_w.shape),
            _const_spec((1, LRU_WIDTH)),
            _const_spec(w_gate.shape),
            _const_spec(b_gate.shape),
            _const_spec((1, LRU_WIDTH)),
            _layer_spec(mkv.shape, 0),
            _layer_spec(mvt.shape, 0),
            _layer_spec(w_out_all.shape, 0),
            _const_spec((1, D_MODEL)),
        ],
        out_specs=pl.BlockSpec((BATCH, ts, D_MODEL), lambda i: (0, i, 0)),
        out_shape=jax.ShapeDtypeStruct((BATCH, SEQ, D_MODEL), F32),
        scratch_shapes=[
            pltpu.VMEM((BATCH, ts + SUBLANES, LRU_WIDTH), F32),
            pltpu.VMEM((rows, LRU_WIDTH), F32),
            pltpu.VMEM((LRU_BLOCKS, BATCH * pitch, LRU_BLOCK), F32),
            pltpu.VMEM((LRU_BLOCKS, BATCH * pitch, LRU_BLOCK), F32),
            pltpu.VMEM((MEM_WIDTH, rows), BF16),
            pltpu.VMEM((2, MEM_HEADS, N_MEM, ts), F32),
            pltpu.VMEM((MEM_WIDTH, rows), F32),
            pltpu.VMEM((rows, D_MODEL), BF16),
            pltpu.VMEM((BATCH, LRU_WIDTH), F32),
        ],
        compiler_params=pltpu.CompilerParams(dimension_semantics=("arbitrary",),
                                             vmem_limit_bytes=VMEM_LIMIT),
        name="mixer_a",
    )(x, gpre.reshape(1, D_MODEL), w_in, w_q_t, conv_w, conv_b.reshape(1, LRU_WIDTH), w_gate, b_gate,
      lam.reshape(1, LRU_WIDTH), mkv, mvt, w_out_all, gpost.reshape(1, D_MODEL))


def _ffn_body(h_ref, gpre_ref, win_ref, wout_ref, gpost_ref):
    h = h_ref[...]
    hn = _rms(h, gpre_ref[...]).astype(BF16)
    acc = jnp.zeros((h.shape[0], D_MODEL), F32)
    for ck in range(D_FF // FF_CHUNK):
        lo = ck * FF_CHUNK
        g = _dot(hn, win_ref[:, lo:lo + FF_CHUNK].astype(BF16))
        u = _dot(hn, win_ref[:, D_FF + lo:D_FF + lo + FF_CHUNK].astype(BF16))
        act = (g * jax.nn.sigmoid(g) * u).astype(BF16)
        acc = acc + _dot(act, wout_ref[lo:lo + FF_CHUNK, :].astype(BF16))
    return h + _rms(acc, gpost_ref[...])


def _ffn_kernel(h_ref, gpre_ref, win_ref, wout_ref, gpost_ref, o_ref):
    o_ref[...] = _ffn_body(h_ref, gpre_ref, win_ref, wout_ref, gpost_ref)


def _ffn_kv_kernel(h_ref, gpre_ref, win_ref, wout_ref, gpost_ref, gkv_ref, wk_ref, wvt_ref, cos_ref, sin_ref,
                   o_ref, k_ref, vt_ref):
    h_new = _ffn_body(h_ref, gpre_ref, win_ref, wout_ref, gpost_ref)
    o_ref[...] = h_new
    hn = _rms(h_new, gkv_ref[...]).astype(BF16)
    k = _dot(hn, wk_ref[...])
    cos, sin = cos_ref[...], sin_ref[...]
    for j in range(KV_WIDTH // LANES):
        k_ref[:, j * LANES:(j + 1) * LANES] = _rope(k[:, j * LANES:(j + 1) * LANES], cos, sin).astype(BF16)
    vt_ref[0] = _dot_nt(wvt_ref[...], hn).astype(BF16)


def _ffn(h2d, layer, gpre, w_in, w_out, gpost, kv_args=None):
    tokens = h2d.shape[0]
    tm = TM_FFN
    row_spec = pl.BlockSpec((tm, D_MODEL), lambda i: (i, 0))
    in_specs = [row_spec, _const_spec((1, D_MODEL)), _layer_spec(w_in.shape, layer),
                _layer_spec(w_out.shape, layer), _const_spec((1, D_MODEL))]
    args = [h2d, gpre.reshape(1, D_MODEL), w_in, w_out, gpost.reshape(1, D_MODEL)]
    params = pltpu.CompilerParams(dimension_semantics=("arbitrary",), vmem_limit_bytes=VMEM_LIMIT)
    if kv_args is None:
        return pl.pallas_call(
            _ffn_kernel, grid=(tokens // tm,), in_specs=in_specs, out_specs=row_spec,
            out_shape=jax.ShapeDtypeStruct((tokens, D_MODEL), F32),
            compiler_params=params, name="ffn",
        )(*args)
    gkv, wk, wvt, cos, sin = kv_args
    pos_blocks = SEQ // tm
    table_spec = pl.BlockSpec((tm, LANES), lambda i: (i % pos_blocks, 0))
    k_spec = pl.BlockSpec((tm, KV_WIDTH), lambda i: (i, 0))
    vt_spec = pl.BlockSpec((1, KV_WIDTH, tm), lambda i: (i // pos_blocks, 0, i % pos_blocks))
    return pl.pallas_call(
        _ffn_kv_kernel, grid=(tokens // tm,),
        in_specs=in_specs + [_const_spec((1, D_MODEL)), _const_spec(wk.shape), _const_spec(wvt.shape),
                             table_spec, table_spec],
        out_specs=[row_spec, k_spec, vt_spec],
        out_shape=[jax.ShapeDtypeStruct((tokens, D_MODEL), F32),
                   jax.ShapeDtypeStruct((tokens, KV_WIDTH), BF16),
                   jax.ShapeDtypeStruct((BATCH, KV_WIDTH, SEQ), BF16)],
        compiler_params=params, name="ffn_kv",
    )(*args, gkv.reshape(1, D_MODEL), wk, wvt, cos, sin)


def _mixer_b_kernel(sinks_ref, x_ref, gpre_ref, wint_ref, cost_ref, sint_ref, k_ref, vt_ref, mkv_ref, mvt_ref,
                    wout_ref, gpost_ref, o_ref, qt_s, s_s, sm_s, catt_s, bound_s):
    b = pl.program_id(0)
    j = pl.program_id(1)
    tq = TQ_B
    n_qb = tq // WINDOW
    q_width = SWA_Q_HEADS * HEAD_DIM
    grp_rows = SWA_GROUP * HEAD_DIM
    tc = tq // PROJ_CHUNKS_B
    for c in range(PROJ_CHUNKS_B):
        ts_ = slice(c * tc, (c + 1) * tc)
        hn = _rms(x_ref[0, ts_, :], gpre_ref[...]).astype(BF16)
        cos_t, sin_t = cost_ref[:, ts_], sint_ref[:, ts_]
        for kh in range(SWA_KV_HEADS):
            pt = _dot_nt(wint_ref[kh * grp_rows:(kh + 1) * grp_rows, :], hn)
            for g in range(SWA_GROUP):
                r0 = g * HEAD_DIM
                x1, x2 = pt[r0:r0 + ROPE_HALF], pt[r0 + ROPE_HALF:r0 + ROPE_DIM]
                head = jnp.concatenate([x1 * cos_t - x2 * sin_t, x2 * cos_t + x1 * sin_t,
                                        pt[r0 + ROPE_DIM:r0 + HEAD_DIM]], axis=0)
                qt_s[kh * grp_rows + r0:kh * grp_rows + r0 + HEAD_DIM, ts_] = (head * ATTN_SCALE).astype(BF16)
        qt_s[q_width:, ts_] = (_dot_nt(wint_ref[q_width:, :], hn) * ATTN_SCALE).astype(BF16)

    @pl.when(jnp.logical_and(b == 0, j == 0))
    def _():
        ci = lax.broadcasted_iota(jnp.int32, (2 * WINDOW, SWA_GROUP * WINDOW), 0)
        qi = lax.broadcasted_iota(jnp.int32, (2 * WINDOW, SWA_GROUP * WINDOW), 1) & (WINDOW - 1)
        dist = ci - qi
        band = jnp.logical_and(dist > 0, dist <= WINDOW)
        bound_s[0] = jnp.where(band, jnp.inf, NEG_INF)
        bound_s[1] = jnp.where(jnp.logical_and(band, ci >= WINDOW), jnp.inf, NEG_INF)

    def starts(qb):
        start = pl.multiple_of(j * tq + qb * WINDOW, WINDOW)
        return pl.multiple_of(jnp.maximum(start - WINDOW, 0), WINDOW), start

    def swa_scores(qb, kh):
        pstart, start = starts(qb)
        slab = slice((kh // 2) * LANES, (kh // 2 + 1) * LANES)
        band = jnp.concatenate([k_ref[0, pl.ds(pstart, WINDOW), slab], k_ref[0, pl.ds(start, WINDOW), slab]],
                               axis=0)
        qs = slice(qb * WINDOW, (qb + 1) * WINDOW)
        qcat = jnp.concatenate([qt_s[(kh * SWA_GROUP + g) * HEAD_DIM:(kh * SWA_GROUP + g + 1) * HEAD_DIM, qs]
                                for g in range(SWA_GROUP)], axis=1)
        s_s[qb % 2, kh] = _dot(band, _pad_head_rows(qcat, kh % 2))

    def swa_output(qb, kh):
        pstart, start = starts(qb)
        bound = bound_s[jnp.where(j > 0, 0, 1)] if qb == 0 else bound_s[0]
        s = jnp.minimum(s_s[qb % 2, kh], bound)
        sink = jnp.concatenate([jnp.full((1, WINDOW), sinks_ref[kh * SWA_GROUP + g], F32)
                                for g in range(SWA_GROUP)], axis=1)
        m = jnp.maximum(jnp.max(s, axis=0, keepdims=True), sink)
        e = jnp.exp(s - m)
        den = jnp.sum(e, axis=0, keepdims=True) + jnp.exp(sink - m)
        rows = slice(kh * HEAD_DIM, (kh + 1) * HEAD_DIM)
        vband = jnp.concatenate([vt_ref[0, rows, pl.ds(pstart, WINDOW)], vt_ref[0, rows, pl.ds(start, WINDOW)]],
                                axis=1)
        ot = _dot(vband, e.astype(BF16)) * (1.0 / den)
        for g in range(SWA_GROUP):
            h = kh * SWA_GROUP + g
            catt_s[h * HEAD_DIM:(h + 1) * HEAD_DIM, qb * WINDOW:(qb + 1) * WINDOW] = (
                ot[:, g * WINDOW:(g + 1) * WINDOW].astype(BF16))

    def mem_scores(h):
        slab = slice((h // 2) * LANES, (h // 2 + 1) * LANES)
        q = qt_s[q_width + h * HEAD_DIM:q_width + (h + 1) * HEAD_DIM, :]
        sm_s[h] = _dot(mkv_ref[b, :, slab], _pad_head_rows(q, h % 2))

    def mem_output(h):
        s = sm_s[h]
        m = jnp.max(s, axis=0, keepdims=True)
        e = jnp.exp(s - m)
        den = jnp.sum(e, axis=0, keepdims=True)
        ot = _dot(mvt_ref[b, h * HEAD_DIM:(h + 1) * HEAD_DIM, :], e.astype(BF16)) * (1.0 / den)
        catt_s[q_width + h * HEAD_DIM:q_width + (h + 1) * HEAD_DIM, :] = ot.astype(BF16)

    for kh in range(SWA_KV_HEADS):
        swa_scores(0, kh)
    for qb in range(n_qb):
        for kh in range(SWA_KV_HEADS):
            if qb + 1 < n_qb:
                swa_scores(qb + 1, kh)
            else:
                mem_scores(kh)
            swa_output(qb, kh)
    for h in range(MEM_HEADS):
        mem_output(h)

    for c in range(PROJ_CHUNKS_B):
        ts_ = slice(c * tc, (c + 1) * tc)
        mixed = lax.dot_general(catt_s[:, ts_], wout_ref[...], (((0,), (0,)), ((), ())),
                                preferred_element_type=F32)
        o_ref[0, ts_, :] = x_ref[0, ts_, :] + _rms(mixed, gpost_ref[...])


def _mixer_b(h, sinks, gpre, w_in_t, cos_t, sin_t, k, vt, mkv, mvt, w_out_all, gpost):
    tq = TQ_B
    grid_spec = pltpu.PrefetchScalarGridSpec(
        num_scalar_prefetch=1,
        grid=(BATCH, SEQ // tq),
        in_specs=[
            pl.BlockSpec((1, tq, D_MODEL), lambda b, j, s: (b, j, 0)),
            _const_spec((1, D_MODEL)),
            _const_spec(w_in_t.shape),
            pl.BlockSpec((ROPE_HALF, tq), lambda b, j, s: (0, j)),
            pl.BlockSpec((ROPE_HALF, tq), lambda b, j, s: (0, j)),
            pl.BlockSpec((1, SEQ, KV_WIDTH), lambda b, j, s: (b, 0, 0)),
            pl.BlockSpec((1, KV_WIDTH, SEQ), lambda b, j, s: (b, 0, 0)),
            _layer_spec(mkv.shape, 1),
            _layer_spec(mvt.shape, 1),
            _layer_spec(w_out_all.shape, 1),
            _const_spec((1, D_MODEL)),
        ],
        out_specs=pl.BlockSpec((1, tq, D_MODEL), lambda b, j, s: (b, j, 0)),
        scratch_shapes=[
            pltpu.VMEM((D_MODEL, tq), BF16),
            pltpu.VMEM((2, SWA_KV_HEADS, 2 * WINDOW, SWA_GROUP * WINDOW), F32),
            pltpu.VMEM((MEM_HEADS, N_MEM, tq), F32),
            pltpu.VMEM((D_MODEL, tq), BF16),
            pltpu.VMEM((2, 2 * WINDOW, SWA_GROUP * WINDOW), F32),
        ],
    )
    return pl.pallas_call(
        _mixer_b_kernel,
        grid_spec=grid_spec,
        out_shape=jax.ShapeDtypeStruct((BATCH, SEQ, D_MODEL), F32),
        compiler_params=pltpu.CompilerParams(dimension_semantics=("arbitrary", "arbitrary"),
                                             vmem_limit_bytes=VMEM_LIMIT),
        name="mixer_b",
    )(sinks, h, gpre.reshape(1, D_MODEL), w_in_t, cos_t, sin_t, k, vt, mkv, mvt, w_out_all,
      gpost.reshape(1, D_MODEL))


def _rope_tables():
    inv_freq = 1.0 / (ROPE_THETA ** (jnp.arange(0, ROPE_DIM, 2, dtype=F32) / ROPE_DIM))
    ang = jnp.arange(SEQ, dtype=F32)[:, None] * inv_freq[None, :]
    c, s = jnp.cos(ang), jnp.sin(ang)
    rest = HEAD_DIM - ROPE_DIM
    cos_h = jnp.concatenate([c, c, jnp.ones((SEQ, rest), F32)], axis=-1)
    sin_h = jnp.concatenate([-s, s, jnp.zeros((SEQ, rest), F32)], axis=-1)
    reps = LANES // HEAD_DIM
    return jnp.tile(cos_h, (1, reps)), jnp.tile(sin_h, (1, reps)), c.T, s.T


def kernel(x, mem, norm_mix_pre, norm_mix_post, norm_ffn_pre, norm_ffn_post, mem_norm, w_mem_kv, w_in_a,
           conv_w, conv_b, w_gate_r, b_gate_r, w_gate_i, b_gate_i, lru_lambda, norm_kv, w_kv_shared, w_in_b,
           sinks, w_out, w_ffn_in, w_ffn_out):
    tokens = BATCH * SEQ
    cos, sin, cos_t, sin_t = _rope_tables()
    w_gate = jnp.concatenate([w_gate_r[0], w_gate_i[0]], axis=-1).astype(BF16)
    b_gate = jnp.concatenate([b_gate_r[0].reshape(LRU_BLOCKS, 1, LRU_BLOCK),
                              b_gate_i[0].reshape(LRU_BLOCKS, 1, LRU_BLOCK)], axis=-1)

    w_out_bf = w_out.astype(BF16)
    w_mem_vt = jnp.swapaxes(w_mem_kv[:, :, MEM_WIDTH:], 1, 2).astype(BF16)
    mkv, mvt = _memkv(mem, mem_norm, w_mem_kv.astype(BF16), w_mem_vt)

    h = _mixer_a(x, norm_mix_pre[0], w_in_a[0, :, :2 * LRU_WIDTH].astype(BF16),
                 w_in_a[0, :, 2 * LRU_WIDTH:].T.astype(BF16), conv_w[0], conv_b[0], w_gate, b_gate,
                 lru_lambda[0], mkv, mvt, w_out_bf, norm_mix_post[0])
    h2d, k, vt = _ffn(h.reshape(tokens, D_MODEL), 0, norm_ffn_pre[0], w_ffn_in, w_ffn_out, norm_ffn_post[0],
                      kv_args=(norm_kv, w_kv_shared[:, :KV_WIDTH].astype(BF16),
                               w_kv_shared[:, KV_WIDTH:].T.astype(BF16), cos, sin))

    h = _mixer_b(h2d.reshape(BATCH, SEQ, D_MODEL), sinks[0], norm_mix_pre[1], w_in_b[0].T.astype(BF16),
                 cos_t, sin_t, k.reshape(BATCH, SEQ, KV_WIDTH), vt, mkv, mvt, w_out_bf, norm_mix_post[1])
    h2d = _ffn(h.reshape(tokens, D_MODEL), 1, norm_ffn_pre[1], w_ffn_in, w_ffn_out, norm_ffn_post[1])
    return h2d.reshape(BATCH, SEQ, D_MODEL)
```

```python
import functools
import math

import jax
import jax.numpy as jnp
from jax import lax
from jax.experimental import pallas as pl
from jax.experimental.pallas import tpu as pltpu

D_MODEL = 1024
BATCH = 8
SEQ = 4096
HEAD_DIM = 64
MEM_HEADS = 4
MEM_WIDTH = MEM_HEADS * HEAD_DIM
N_MEM = 256
LRU_WIDTH = D_MODEL - MEM_WIDTH
LRU_BLOCKS = 6
LRU_BLOCK = LRU_WIDTH // LRU_BLOCKS
CONV_WIDTH = 4
LRU_C = 8.0
SWA_Q_HEADS = 12
SWA_KV_HEADS = 4
SWA_GROUP = SWA_Q_HEADS // SWA_KV_HEADS
KV_WIDTH = SWA_KV_HEADS * HEAD_DIM
WINDOW = 128
ROPE_DIM = HEAD_DIM // 4
ROPE_HALF = ROPE_DIM // 2
ROPE_THETA = 500000.0
D_FF = 2816
EPS = 1e-6
NEG_INF = -1e30
ATTN_SCALE = HEAD_DIM ** -0.5

LANES = 128
SUBLANES = 8
VMEM_LIMIT = 56 * 1024 * 1024

TS_A = 128
PITCH_A = TS_A + SUBLANES
TM_FFN = 512
FF_CHUNK = 256
TQ_B = 512
PROJ_CHUNKS_B = 2
OUT_CHUNKS_A = 4

BF16 = jnp.bfloat16
F32 = jnp.float32


def _rms(x, g):
    ms = jnp.mean(x * x, axis=-1, keepdims=True)
    return x * lax.rsqrt(ms + EPS) * g


def _dot(a, b):
    return jnp.dot(a, b, preferred_element_type=F32)


def _dot_nt(a, b):
    return lax.dot_general(a, b, (((1,), (1,)), ((), ())), preferred_element_type=F32)


def _gelu_tanh(x):
    k1 = -2.0 * math.sqrt(2.0 / math.pi) * math.log2(math.e)
    return x / (1.0 + jnp.exp2(x * (k1 + (k1 * 0.044715) * (x * x))))


def _rope(x, cos, sin):
    lane = lax.broadcasted_iota(jnp.int32, x.shape, 1) & (HEAD_DIM - 1)
    partner = jnp.where(lane < ROPE_HALF,
                        pltpu.roll(x, LANES - ROPE_HALF, 1),
                        pltpu.roll(x, ROPE_HALF, 1))
    return x * cos + partner * sin


def _memkv_kernel(mem_ref, g_ref, w_ref, wvt_ref, o_ref, ovt_ref):
    mem_n = _rms(mem_ref[0], g_ref[...]).astype(BF16)
    for l in range(2):
        o_ref[l, 0] = _dot(mem_n, w_ref[l]).astype(BF16)
        ovt_ref[l, 0] = _dot_nt(wvt_ref[l], mem_n).astype(BF16)


def _memkv(mem, mem_norm, w_mem_kv, w_mem_vt):
    return pl.pallas_call(
        _memkv_kernel,
        grid=(BATCH,),
        in_specs=[
            pl.BlockSpec((1, N_MEM, D_MODEL), lambda b: (b, 0, 0)),
            pl.BlockSpec((1, D_MODEL), lambda b: (0, 0)),
            pl.BlockSpec((2, D_MODEL, 2 * MEM_WIDTH), lambda b: (0, 0, 0)),
            pl.BlockSpec((2, MEM_WIDTH, D_MODEL), lambda b: (0, 0, 0)),
        ],
        out_specs=[pl.BlockSpec((2, 1, N_MEM, 2 * MEM_WIDTH), lambda b: (0, b, 0, 0)),
                   pl.BlockSpec((2, 1, MEM_WIDTH, N_MEM), lambda b: (0, b, 0, 0))],
        out_shape=[jax.ShapeDtypeStruct((2, BATCH, N_MEM, 2 * MEM_WIDTH), BF16),
                   jax.ShapeDtypeStruct((2, BATCH, MEM_WIDTH, N_MEM), BF16)],
        compiler_params=pltpu.CompilerParams(dimension_semantics=("arbitrary",),
                                             vmem_limit_bytes=VMEM_LIMIT),
        name="memkv",
    )(mem, mem_norm.reshape(1, D_MODEL), w_mem_kv, w_mem_vt)


def _const_spec(shape):
    nd = len(shape)
    return pl.BlockSpec(shape, lambda *_: (0,) * nd, pipeline_mode=pl.Buffered(1))


def _layer_spec(stacked_shape, layer):
    nd = len(stacked_shape) - 1
    return pl.BlockSpec((None,) + tuple(stacked_shape[1:]), lambda *_: (layer,) + (0,) * nd,
                        pipeline_mode=pl.Buffered(1))


def _pad_head_rows(q, half):
    z = jnp.zeros_like(q)
    return jnp.concatenate([z, q] if half else [q, z], axis=0)


def _mixer_a_kernel(x_ref, gpre_ref, win_ref, wqt_ref, convw_ref, convb_ref, wg_ref, bg_ref, lam_ref,
                    mkv_ref, mvt_ref, wout_ref, gpost_ref, o_ref,
                    xr_s, xc_s, a_s, u_s, qt_s, sm_s, mt_s, cat_s, h_s):
    i = pl.program_id(0)
    ts, pitch = TS_A, PITCH_A
    rows = BATCH * ts

    @pl.when(i == 0)
    def _():
        h_s[...] = jnp.zeros_like(h_s)
        xr_s[...] = jnp.zeros_like(xr_s)

    hist = ts + SUBLANES
    nb = BATCH // OUT_CHUNKS_A
    hn_parts = []
    for rc in range(OUT_CHUNKS_A):
        x_c = x_ref[rc * nb:(rc + 1) * nb].reshape(nb * ts, D_MODEL)
        hn_c = _rms(x_c, gpre_ref[...]).astype(BF16)
        hn_parts.append(hn_c)
        xr = _dot(hn_c, win_ref[:, 0:LRU_WIDTH])
        for c in range(LRU_BLOCKS):
            cs = slice(c * LRU_BLOCK, (c + 1) * LRU_BLOCK)
            for bb in range(nb):
                b = rc * nb + bb
                xr_s[c, b * hist + SUBLANES:(b + 1) * hist, :] = xr[bb * ts:(bb + 1) * ts, cs]
    hn = jnp.concatenate(hn_parts, axis=0)

    qt_s[...] = (_dot_nt(wqt_ref[...], hn) * ATTN_SCALE).astype(BF16)

    for c in range(LRU_BLOCKS):
        cs = slice(c * LRU_BLOCK, (c + 1) * LRU_BLOCK)
        for b in range(BATCH):
            acc = jnp.broadcast_to(convb_ref[:, cs], (ts, LRU_BLOCK))
            for k in range(CONV_WIDTH):
                off = b * hist + SUBLANES - (CONV_WIDTH - 1) + k
                acc = acc + xr_s[c, off:off + ts, :] * convw_ref[k:k + 1, cs]
            xc_s[c, b * ts:(b + 1) * ts, :] = acc
            xr_s[c, b * hist:b * hist + SUBLANES, :] = xr_s[c, b * hist + ts:(b + 1) * hist, :]

    lam = lam_ref[...]
    log_sig = jnp.minimum(lam, 0.0) - jnp.log1p(jnp.exp(-jnp.abs(lam)))
    ls_c = (LRU_C * math.log2(math.e)) * log_sig
    tpos = lax.broadcasted_iota(jnp.int32, (rows, 1), 0) & (ts - 1)
    seq_start = jnp.logical_and(tpos == 0, i == 0)
    for c in range(LRU_BLOCKS):
        cs = slice(c * LRU_BLOCK, (c + 1) * LRU_BLOCK)
        xc = xc_s[c]
        g = _dot(xc.astype(BF16), wg_ref[c]) + bg_ref[c]
        r = jax.nn.sigmoid(g[:, :LRU_BLOCK])
        ig = jax.nn.sigmoid(g[:, LRU_BLOCK:])
        a = jnp.exp2(r * ls_c[:, cs])
        t = 1.0 - a * a
        mult = jnp.where(t > 0.0, t * lax.rsqrt(t), 0.0)
        mult = jnp.where(seq_start, 1.0, mult)
        u = mult * (ig * xc)
        for b in range(BATCH):
            a_s[c, b * pitch:b * pitch + ts, :] = a[b * ts:(b + 1) * ts]
            u_s[c, b * pitch:b * pitch + ts, :] = u[b * ts:(b + 1) * ts]

        if c % 2 == 1:
            gs = slice((c - 1) * LRU_BLOCK, (c + 1) * LRU_BLOCK)
            gate = _gelu_tanh(_dot(hn, win_ref[:, LRU_WIDTH + gs.start:LRU_WIDTH + gs.stop]))
            xc_s[c - 1] = gate[:, :LRU_BLOCK]
            xc_s[c] = gate[:, LRU_BLOCK:]


    def mem_scores(b, h):
        slab = slice((h // 2) * LANES, (h // 2 + 1) * LANES)
        q = qt_s[h * HEAD_DIM:(h + 1) * HEAD_DIM, b * ts:(b + 1) * ts]
        sm_s[b % 2, h] = _dot(mkv_ref[b, :, slab], _pad_head_rows(q, h % 2))

    def mem_output(b, h):
        s = sm_s[b % 2, h]
        m = jnp.max(s, axis=0, keepdims=True)
        e = jnp.exp(s - m)
        den = jnp.sum(e, axis=0, keepdims=True)
        ot = _dot(mvt_ref[b, h * HEAD_DIM:(h + 1) * HEAD_DIM, :], e.astype(BF16)) * (1.0 / den)
        mt_s[h * HEAD_DIM:(h + 1) * HEAD_DIM, b * ts:(b + 1) * ts] = ot

    h = [h_s[:, c * LRU_BLOCK:(c + 1) * LRU_BLOCK] for c in range(LRU_BLOCKS)]
    steps_per_chunk = ts // BATCH
    for hd in range(MEM_HEADS):
        mem_scores(0, hd)
    for b in range(BATCH):
        for t in range(b * steps_per_chunk, (b + 1) * steps_per_chunk):
            idx = pl.ds(t, BATCH, stride=pitch)
            for c in range(LRU_BLOCKS):
                h[c] = a_s[c, idx, :] * h[c] + u_s[c, idx, :]
                u_s[c, idx, :] = h[c]
        for hd in range(MEM_HEADS):
            if b + 1 < BATCH:
                mem_scores(b + 1, hd)
            mem_output(b, hd)
    for c in range(LRU_BLOCKS):
        h_s[:, c * LRU_BLOCK:(c + 1) * LRU_BLOCK] = h[c]

    nb = BATCH // OUT_CHUNKS_A
    for oc in range(OUT_CHUNKS_A):
        for b in range(oc * nb, (oc + 1) * nb):
            rs = slice(b * ts, (b + 1) * ts)
            for c in range(LRU_BLOCKS):
                cs = slice(c * LRU_BLOCK, (c + 1) * LRU_BLOCK)
                cat_s[rs, cs] = (u_s[c, b * pitch:b * pitch + ts, :] * xc_s[c, rs, :]).astype(BF16)
            cat_s[rs, LRU_WIDTH:] = mt_s[:, rs].T.astype(BF16)
        rows_c = slice(oc * nb * ts, (oc + 1) * nb * ts)
        mixed = _dot(cat_s[rows_c, :], wout_ref[...])
        x_c = x_ref[oc * nb:(oc + 1) * nb].reshape(nb * ts, D_MODEL)
        o_ref[oc * nb:(oc + 1) * nb] = (x_c + _rms(mixed, gpost_ref[...])).reshape(nb, ts, D_MODEL)


def _mixer_a(x, gpre, w_in, w_q_t, conv_w, conv_b, w_gate, b_gate, lam, mkv, mvt, w_out_all, gpost):
    ts, pitch = TS_A, PITCH_A
    rows = BATCH * ts
    return pl.pallas_call(
        _mixer_a_kernel,
        grid=(SEQ // ts,),
        in_specs=[
            pl.BlockSpec((BATCH, ts, D_MODEL), lambda i: (0, i, 0)),
            _const_spec((1, D_MODEL)),
            _const_spec(w_in.shape),
            _const_spec(w_q_t.shape),
            _const_spec(conv_w.shape),
            _const_spec((1, LRU_WIDTH)),
            _const_spec(w_gate.shape),
            _const_spec(b_gate.shape),
            _const_spec((1, LRU_WIDTH)),
            _layer_spec(mkv.shape, 0),
            _layer_spec(mvt.shape, 0),
            _layer_spec(w_out_all.shape, 0),
            _const_spec((1, D_MODEL)),
        ],
        out_specs=pl.BlockSpec((BATCH, ts, D_MODEL), lambda i: (0, i, 0)),
        out_shape=jax.ShapeDtypeStruct((BATCH, SEQ, D_MODEL), F32),
        scratch_shapes=[
            pltpu.VMEM((LRU_BLOCKS, BATCH * (ts + SUBLANES), LRU_BLOCK), F32),
            pltpu.VMEM((LRU_BLOCKS, rows, LRU_BLOCK), F32),
            pltpu.VMEM((LRU_BLOCKS, BATCH * pitch, LRU_BLOCK), F32),
            pltpu.VMEM((LRU_BLOCKS, BATCH * pitch, LRU_BLOCK), F32),
            pltpu.VMEM((MEM_WIDTH, rows), BF16),
            pltpu.VMEM((2, MEM_HEADS, N_MEM, ts), F32),
            pltpu.VMEM((MEM_WIDTH, rows), F32),
            pltpu.VMEM((rows, D_MODEL), BF16),
            pltpu.VMEM((BATCH, LRU_WIDTH), F32),
        ],
        compiler_params=pltpu.CompilerParams(dimension_semantics=("arbitrary",),
                                             vmem_limit_bytes=VMEM_LIMIT),
        name="mixer_a",
    )(x, gpre.reshape(1, D_MODEL), w_in, w_q_t, conv_w, conv_b.reshape(1, LRU_WIDTH), w_gate, b_gate,
      lam.reshape(1, LRU_WIDTH), mkv, mvt, w_out_all, gpost.reshape(1, D_MODEL))


def _ffn_body(h_ref, gpre_ref, win_ref, wout_ref, gpost_ref):
    h = h_ref[...]
    hn = _rms(h, gpre_ref[...]).astype(BF16)
    acc = jnp.zeros((h.shape[0], D_MODEL), F32)
    for ck in range(D_FF // FF_CHUNK):
        lo = ck * FF_CHUNK
        g = _dot(hn, win_ref[:, lo:lo + FF_CHUNK].astype(BF16))
        u = _dot(hn, win_ref[:, D_FF + lo:D_FF + lo + FF_CHUNK].astype(BF16))
        act = (g * jax.nn.sigmoid(g) * u).astype(BF16)
        acc = acc + _dot(act, wout_ref[lo:lo + FF_CHUNK, :].astype(BF16))
    return h + _rms(acc, gpost_ref[...])


def _ffn_kernel(h_ref, gpre_ref, win_ref, wout_ref, gpost_ref, o_ref):
    o_ref[...] = _ffn_body(h_ref, gpre_ref, win_ref, wout_ref, gpost_ref)


def _ffn_kv_kernel(h_ref, gpre_ref, win_ref, wout_ref, gpost_ref, gkv_ref, wk_ref, wvt_ref, cos_ref, sin_ref,
                   o_ref, k_ref, vt_ref):
    h_new = _ffn_body(h_ref, gpre_ref, win_ref, wout_ref, gpost_ref)
    o_ref[...] = h_new
    hn = _rms(h_new, gkv_ref[...]).astype(BF16)
    k = _dot(hn, wk_ref[...])
    cos, sin = cos_ref[...], sin_ref[...]
    for j in range(KV_WIDTH // LANES):
        k_ref[:, j * LANES:(j + 1) * LANES] = _rope(k[:, j * LANES:(j + 1) * LANES], cos, sin).astype(BF16)
    vt_ref[0] = _dot_nt(wvt_ref[...], hn).astype(BF16)


def _ffn(h2d, layer, gpre, w_in, w_out, gpost, kv_args=None):
    tokens = h2d.shape[0]
    tm = TM_FFN
    row_spec = pl.BlockSpec((tm, D_MODEL), lambda i: (i, 0))
    in_specs = [row_spec, _const_spec((1, D_MODEL)), _layer_spec(w_in.shape, layer),
                _layer_spec(w_out.shape, layer), _const_spec((1, D_MODEL))]
    args = [h2d, gpre.reshape(1, D_MODEL), w_in, w_out, gpost.reshape(1, D_MODEL)]
    params = pltpu.CompilerParams(dimension_semantics=("arbitrary",), vmem_limit_bytes=VMEM_LIMIT)
    if kv_args is None:
        return pl.pallas_call(
            _ffn_kernel, grid=(tokens // tm,), in_specs=in_specs, out_specs=row_spec,
            out_shape=jax.ShapeDtypeStruct((tokens, D_MODEL), F32),
            compiler_params=params, name="ffn",
        )(*args)
    gkv, wk, wvt, cos, sin = kv_args
    pos_blocks = SEQ // tm
    table_spec = pl.BlockSpec((tm, LANES), lambda i: (i % pos_blocks, 0))
    k_spec = pl.BlockSpec((tm, KV_WIDTH), lambda i: (i, 0))
    vt_spec = pl.BlockSpec((1, KV_WIDTH, tm), lambda i: (i // pos_blocks, 0, i % pos_blocks))
    return pl.pallas_call(
        _ffn_kv_kernel, grid=(tokens // tm,),
        in_specs=in_specs + [_const_spec((1, D_MODEL)), _const_spec(wk.shape), _const_spec(wvt.shape),
                             table_spec, table_spec],
        out_specs=[row_spec, k_spec, vt_spec],
        out_shape=[jax.ShapeDtypeStruct((tokens, D_MODEL), F32),
                   jax.ShapeDtypeStruct((tokens, KV_WIDTH), BF16),
                   jax.ShapeDtypeStruct((BATCH, KV_WIDTH, SEQ), BF16)],
        compiler_params=params, name="ffn_kv",
    )(*args, gkv.reshape(1, D_MODEL), wk, wvt, cos, sin)


def _mixer_b_kernel(sinks_ref, x_ref, gpre_ref, wint_ref, cost_ref, sint_ref, k_ref, vt_ref, mkv_ref, mvt_ref,
                    wout_ref, gpost_ref, o_ref, qt_s, s_s, sm_s, catt_s, bound_s, e_s):
    b = pl.program_id(0)
    j = pl.program_id(1)
    tq = TQ_B
    n_qb = tq // WINDOW
    q_width = SWA_Q_HEADS * HEAD_DIM
    grp_rows = SWA_GROUP * HEAD_DIM
    tc = tq // PROJ_CHUNKS_B
    for c in range(PROJ_CHUNKS_B):
        ts_ = slice(c * tc, (c + 1) * tc)
        hn = _rms(x_ref[0, ts_, :], gpre_ref[...]).astype(BF16)
        cos_t, sin_t = cost_ref[:, ts_], sint_ref[:, ts_]
        for kh in range(SWA_KV_HEADS):
            pt = _dot_nt(wint_ref[kh * grp_rows:(kh + 1) * grp_rows, :], hn)
            for g in range(SWA_GROUP):
                r0 = g * HEAD_DIM
                x1, x2 = pt[r0:r0 + ROPE_HALF], pt[r0 + ROPE_HALF:r0 + ROPE_DIM]
                head = jnp.concatenate([x1 * cos_t - x2 * sin_t, x2 * cos_t + x1 * sin_t,
                                        pt[r0 + ROPE_DIM:r0 + HEAD_DIM]], axis=0)
                qt_s[kh * grp_rows + r0:kh * grp_rows + r0 + HEAD_DIM, ts_] = (head * ATTN_SCALE).astype(BF16)
        qt_s[q_width:, ts_] = (_dot_nt(wint_ref[q_width:, :], hn) * ATTN_SCALE).astype(BF16)

    @pl.when(jnp.logical_and(b == 0, j == 0))
    def _():
        ci = lax.broadcasted_iota(jnp.int32, (2 * WINDOW, WINDOW), 0)
        qi = lax.broadcasted_iota(jnp.int32, (2 * WINDOW, WINDOW), 1)
        dist = ci - qi
        band = jnp.logical_and(dist > 0, dist <= WINDOW)
        bound_s[0] = jnp.where(band, jnp.inf, NEG_INF)
        bound_s[1] = jnp.where(jnp.logical_and(band, ci >= WINDOW), jnp.inf, NEG_INF)

    def starts(qb):
        start = pl.multiple_of(j * tq + qb * WINDOW, WINDOW)
        return pl.multiple_of(jnp.maximum(start - WINDOW, 0), WINDOW), start

    def swa_scores(qb, kh):
        pstart, start = starts(qb)
        slab = slice((kh // 2) * LANES, (kh // 2 + 1) * LANES)
        band = jnp.concatenate([k_ref[0, pl.ds(pstart, WINDOW), slab], k_ref[0, pl.ds(start, WINDOW), slab]],
                               axis=0)
        qs = slice(qb * WINDOW, (qb + 1) * WINDOW)
        qcat = jnp.concatenate([qt_s[(kh * SWA_GROUP + g) * HEAD_DIM:(kh * SWA_GROUP + g + 1) * HEAD_DIM, qs]
                                for g in range(SWA_GROUP)], axis=1)
        s_s[qb % 2, kh] = _dot(band, _pad_head_rows(qcat, kh % 2))

    def swa_output(qb, kh):
        pstart, start = starts(qb)
        bound = bound_s[jnp.where(j > 0, 0, 1)] if qb == 0 else bound_s[0]
        slot = kh % 2
        rden = []
        for g in range(SWA_GROUP):
            gs = slice(g * WINDOW, (g + 1) * WINDOW)
            s = jnp.minimum(s_s[qb % 2, kh, :, gs], bound)
            sink = jnp.full((1, WINDOW), sinks_ref[kh * SWA_GROUP + g], F32)
            m = jnp.maximum(jnp.max(s, axis=0, keepdims=True), sink)
            e = jnp.exp(s - m)
            rden.append(1.0 / (jnp.sum(e, axis=0, keepdims=True) + jnp.exp(sink - m)))
            e_s[slot, :, gs] = e.astype(BF16)
        rows = slice(kh * HEAD_DIM, (kh + 1) * HEAD_DIM)
        vband = jnp.concatenate([vt_ref[0, rows, pl.ds(pstart, WINDOW)], vt_ref[0, rows, pl.ds(start, WINDOW)]],
                                axis=1)
        ot = _dot(vband, e_s[slot, :, 0:SWA_GROUP * WINDOW])
        for g in range(SWA_GROUP):
            h = kh * SWA_GROUP + g
            catt_s[h * HEAD_DIM:(h + 1) * HEAD_DIM, qb * WINDOW:(qb + 1) * WINDOW] = (
                ot[:, g * WINDOW:(g + 1) * WINDOW] * rden[g]).astype(BF16)

    def mem_scores(h):
        slab = slice((h // 2) * LANES, (h // 2 + 1) * LANES)
        q = qt_s[q_width + h * HEAD_DIM:q_width + (h + 1) * HEAD_DIM, :]
        sm_s[h] = _dot(mkv_ref[b, :, slab], _pad_head_rows(q, h % 2))

    def mem_output(h):
        slot = h % 2
        rden = []
        for g in range(tq // LANES):
            gs = slice(g * LANES, (g + 1) * LANES)
            s = sm_s[h, :, gs]
            e = jnp.exp(s - jnp.max(s, axis=0, keepdims=True))
            rden.append(1.0 / jnp.sum(e, axis=0, keepdims=True))
            e_s[slot, :, gs] = e.astype(BF16)
        ot = _dot(mvt_ref[b, h * HEAD_DIM:(h + 1) * HEAD_DIM, :], e_s[slot]) * jnp.concatenate(rden, axis=1)
        catt_s[q_width + h * HEAD_DIM:q_width + (h + 1) * HEAD_DIM, :] = ot.astype(BF16)

    for kh in range(SWA_KV_HEADS):
        swa_scores(0, kh)
    for qb in range(n_qb):
        for kh in range(SWA_KV_HEADS):
            if qb + 1 < n_qb:
                swa_scores(qb + 1, kh)
            else:
                mem_scores(kh)
            swa_output(qb, kh)
    for h in range(MEM_HEADS):
        mem_output(h)

    for c in range(PROJ_CHUNKS_B):
        ts_ = slice(c * tc, (c + 1) * tc)
        mixed = lax.dot_general(catt_s[:, ts_], wout_ref[...], (((0,), (0,)), ((), ())),
                                preferred_element_type=F32)
        o_ref[0, ts_, :] = x_ref[0, ts_, :] + _rms(mixed, gpost_ref[...])


def _mixer_b(h, sinks, gpre, w_in_t, cos_t, sin_t, k, vt, mkv, mvt, w_out_all, gpost):
    tq = TQ_B
    grid_spec = pltpu.PrefetchScalarGridSpec(
        num_scalar_prefetch=1,
        grid=(BATCH, SEQ // tq),
        in_specs=[
            pl.BlockSpec((1, tq, D_MODEL), lambda b, j, s: (b, j, 0)),
            _const_spec((1, D_MODEL)),
            _const_spec(w_in_t.shape),
            pl.BlockSpec((ROPE_HALF, tq), lambda b, j, s: (0, j)),
            pl.BlockSpec((ROPE_HALF, tq), lambda b, j, s: (0, j)),
            pl.BlockSpec((1, SEQ, KV_WIDTH), lambda b, j, s: (b, 0, 0)),
            pl.BlockSpec((1, KV_WIDTH, SEQ), lambda b, j, s: (b, 0, 0)),
            _layer_spec(mkv.shape, 1),
            _layer_spec(mvt.shape, 1),
            _layer_spec(w_out_all.shape, 1),
            _const_spec((1, D_MODEL)),
        ],
        out_specs=pl.BlockSpec((1, tq, D_MODEL), lambda b, j, s: (b, j, 0)),
        scratch_shapes=[
            pltpu.VMEM((D_MODEL, tq), BF16),
            pltpu.VMEM((2, SWA_KV_HEADS, 2 * WINDOW, SWA_GROUP * WINDOW), F32),
            pltpu.VMEM((MEM_HEADS, N_MEM, tq), F32),
            pltpu.VMEM((D_MODEL, tq), BF16),
            pltpu.VMEM((2, 2 * WINDOW, WINDOW), F32),
            pltpu.VMEM((2, N_MEM, tq), BF16),
        ],
    )
    return pl.pallas_call(
        _mixer_b_kernel,
        grid_spec=grid_spec,
        out_shape=jax.ShapeDtypeStruct((BATCH, SEQ, D_MODEL), F32),
        compiler_params=pltpu.CompilerParams(dimension_semantics=("arbitrary", "arbitrary"),
                                             vmem_limit_bytes=VMEM_LIMIT),
        name="mixer_b",
    )(sinks, h, gpre.reshape(1, D_MODEL), w_in_t, cos_t, sin_t, k, vt, mkv, mvt, w_out_all,
      gpost.reshape(1, D_MODEL))


def _rope_tables():
    inv_freq = 1.0 / (ROPE_THETA ** (jnp.arange(0, ROPE_DIM, 2, dtype=F32) / ROPE_DIM))
    ang = jnp.arange(SEQ, dtype=F32)[:, None] * inv_freq[None, :]
    c, s = jnp.cos(ang), jnp.sin(ang)
    rest = HEAD_DIM - ROPE_DIM
    cos_h = jnp.concatenate([c, c, jnp.ones((SEQ, rest), F32)], axis=-1)
    sin_h = jnp.concatenate([-s, s, jnp.zeros((SEQ, rest), F32)], axis=-1)
    reps = LANES // HEAD_DIM
    return jnp.tile(cos_h, (1, reps)), jnp.tile(sin_h, (1, reps)), c.T, s.T


def kernel(x, mem, norm_mix_pre, norm_mix_post, norm_ffn_pre, norm_ffn_post, mem_norm, w_mem_kv, w_in_a,
           conv_w, conv_b, w_gate_r, b_gate_r, w_gate_i, b_gate_i, lru_lambda, norm_kv, w_kv_shared, w_in_b,
           sinks, w_out, w_ffn_in, w_ffn_out):
    tokens = BATCH * SEQ
    cos, sin, cos_t, sin_t = _rope_tables()
    w_gate = jnp.concatenate([w_gate_r[0], w_gate_i[0]], axis=-1).astype(BF16)
    b_gate = jnp.concatenate([b_gate_r[0].reshape(LRU_BLOCKS, 1, LRU_BLOCK),
                              b_gate_i[0].reshape(LRU_BLOCKS, 1, LRU_BLOCK)], axis=-1)

    w_out_bf = w_out.astype(BF16)
    w_mem_vt = jnp.swapaxes(w_mem_kv[:, :, MEM_WIDTH:], 1, 2).astype(BF16)
    mkv, mvt = _memkv(mem, mem_norm, w_mem_kv.astype(BF16), w_mem_vt)

    h = _mixer_a(x, norm_mix_pre[0], w_in_a[0, :, :2 * LRU_WIDTH].astype(BF16),
                 w_in_a[0, :, 2 * LRU_WIDTH:].T.astype(BF16), conv_w[0], conv_b[0], w_gate, b_gate,
                 lru_lambda[0], mkv, mvt, w_out_bf, norm_mix_post[0])
    h2d, k, vt = _ffn(h.reshape(tokens, D_MODEL), 0, norm_ffn_pre[0], w_ffn_in, w_ffn_out, norm_ffn_post[0],
                      kv_args=(norm_kv, w_kv_shared[:, :KV_WIDTH].astype(BF16),
                               w_kv_shared[:, KV_WIDTH:].T.astype(BF16), cos, sin))

    h = _mixer_b(h2d.reshape(BATCH, SEQ, D_MODEL), sinks[0], norm_mix_pre[1], w_in_b[0].T.astype(BF16),
                 cos_t, sin_t, k.reshape(BATCH, SEQ, KV_WIDTH), vt, mkv, mvt, w_out_bf, norm_mix_post[1])
    h2d = _ffn(h.reshape(tokens, D_MODEL), 1, norm_ffn_pre[1], w_ffn_in, w_ffn_out, norm_ffn_post[1])
    return h2d.reshape(BATCH, SEQ, D_MODEL)
```

```python
import functools
import math

import jax
import jax.numpy as jnp
from jax import lax
from jax.experimental import pallas as pl
from jax.experimental.pallas import tpu as pltpu

D_MODEL = 1024
BATCH = 8
SEQ = 4096
HEAD_DIM = 64
MEM_HEADS = 4
MEM_WIDTH = MEM_HEADS * HEAD_DIM
N_MEM = 256
LRU_WIDTH = D_MODEL - MEM_WIDTH
LRU_BLOCKS = 6
LRU_BLOCK = LRU_WIDTH // LRU_BLOCKS
CONV_WIDTH = 4
LRU_C = 8.0
SWA_Q_HEADS = 12
SWA_KV_HEADS = 4
SWA_GROUP = SWA_Q_HEADS // SWA_KV_HEADS
KV_WIDTH = SWA_KV_HEADS * HEAD_DIM
WINDOW = 128
ROPE_DIM = HEAD_DIM // 4
ROPE_HALF = ROPE_DIM // 2
ROPE_THETA = 500000.0
D_FF = 2816
EPS = 1e-6
NEG_INF = -1e30
ATTN_SCALE = HEAD_DIM ** -0.5

LANES = 128
SUBLANES = 8
VMEM_LIMIT = 56 * 1024 * 1024
VMEM_LIMIT_FFN_KV = 62 * 1024 * 1024

TS_A = 128
PITCH_A = TS_A + SUBLANES
TM_FFN = 1024
TM_FFN_KV = 1024
FF_CHUNK = 256
TQ_B = 1024
PROJ_CHUNKS_B = 4
OUT_CHUNKS_A = 4

BF16 = jnp.bfloat16
F32 = jnp.float32


def _rms(x, g):
    ms = jnp.mean(x * x, axis=-1, keepdims=True)
    return x * lax.rsqrt(ms + EPS) * g


def _dot(a, b):
    return jnp.dot(a, b, preferred_element_type=F32)


def _dot_nt(a, b):
    return lax.dot_general(a, b, (((1,), (1,)), ((), ())), preferred_element_type=F32)


def _gelu_tanh(x):
    k1 = -2.0 * math.sqrt(2.0 / math.pi) * math.log2(math.e)
    return x / (1.0 + jnp.exp2(x * (k1 + (k1 * 0.044715) * (x * x))))


def _rope(x, cos, sin):
    lane = lax.broadcasted_iota(jnp.int32, x.shape, 1) & (HEAD_DIM - 1)
    partner = jnp.where(lane < ROPE_HALF,
                        pltpu.roll(x, LANES - ROPE_HALF, 1),
                        pltpu.roll(x, ROPE_HALF, 1))
    return x * cos + partner * sin


def _memkv_kernel(mem_ref, g_ref, w_ref, wvt_ref, o_ref, ovt_ref):
    mem_n = _rms(mem_ref[0], g_ref[...]).astype(BF16)
    for l in range(2):
        o_ref[l, 0] = _dot(mem_n, w_ref[l]).astype(BF16)
        ovt_ref[l, 0] = _dot_nt(wvt_ref[l], mem_n).astype(BF16)


def _memkv(mem, mem_norm, w_mem_kv, w_mem_vt):
    return pl.pallas_call(
        _memkv_kernel,
        grid=(BATCH,),
        in_specs=[
            pl.BlockSpec((1, N_MEM, D_MODEL), lambda b: (b, 0, 0)),
            pl.BlockSpec((1, D_MODEL), lambda b: (0, 0)),
            pl.BlockSpec((2, D_MODEL, 2 * MEM_WIDTH), lambda b: (0, 0, 0)),
            pl.BlockSpec((2, MEM_WIDTH, D_MODEL), lambda b: (0, 0, 0)),
        ],
        out_specs=[pl.BlockSpec((2, 1, N_MEM, 2 * MEM_WIDTH), lambda b: (0, b, 0, 0)),
                   pl.BlockSpec((2, 1, MEM_WIDTH, N_MEM), lambda b: (0, b, 0, 0))],
        out_shape=[jax.ShapeDtypeStruct((2, BATCH, N_MEM, 2 * MEM_WIDTH), BF16),
                   jax.ShapeDtypeStruct((2, BATCH, MEM_WIDTH, N_MEM), BF16)],
        compiler_params=pltpu.CompilerParams(dimension_semantics=("arbitrary",),
                                             vmem_limit_bytes=VMEM_LIMIT),
        name="memkv",
    )(mem, mem_norm.reshape(1, D_MODEL), w_mem_kv, w_mem_vt)


def _const_spec(shape):
    nd = len(shape)
    return pl.BlockSpec(shape, lambda *_: (0,) * nd, pipeline_mode=pl.Buffered(1))


def _layer_spec(stacked_shape, layer):
    nd = len(stacked_shape) - 1
    return pl.BlockSpec((None,) + tuple(stacked_shape[1:]), lambda *_: (layer,) + (0,) * nd,
                        pipeline_mode=pl.Buffered(1))


def _pad_head_rows(q, half):
    z = jnp.zeros_like(q)
    return jnp.concatenate([z, q] if half else [q, z], axis=0)


def _mixer_a_kernel(x_ref, gpre_ref, win_ref, wqt_ref, convw_ref, convb_ref, wg_ref, bg_ref, lam_ref,
                    mkv_ref, mvt_ref, wout_ref, gpost_ref, o_ref,
                    xr_s, xc_s, a_s, u_s, qt_s, sm_s, mt_s, cat_s, h_s):
    i = pl.program_id(0)
    ts, pitch = TS_A, PITCH_A
    rows = BATCH * ts

    @pl.when(i == 0)
    def _():
        h_s[...] = jnp.zeros_like(h_s)
        xr_s[...] = jnp.zeros_like(xr_s)

    hist = ts + SUBLANES
    nb = BATCH // OUT_CHUNKS_A
    hn_parts = []
    for rc in range(OUT_CHUNKS_A):
        x_c = x_ref[rc * nb:(rc + 1) * nb].reshape(nb * ts, D_MODEL)
        hn_c = _rms(x_c, gpre_ref[...]).astype(BF16)
        hn_parts.append(hn_c)
        xr = _dot(hn_c, win_ref[:, 0:LRU_WIDTH])
        for c in range(LRU_BLOCKS):
            cs = slice(c * LRU_BLOCK, (c + 1) * LRU_BLOCK)
            for bb in range(nb):
                b = rc * nb + bb
                xr_s[c, b * hist + SUBLANES:(b + 1) * hist, :] = xr[bb * ts:(bb + 1) * ts, cs]
    hn = jnp.concatenate(hn_parts, axis=0)

    qt_s[...] = (_dot_nt(wqt_ref[...], hn) * ATTN_SCALE).astype(BF16)

    for c in range(LRU_BLOCKS):
        cs = slice(c * LRU_BLOCK, (c + 1) * LRU_BLOCK)
        for b in range(BATCH):
            acc = jnp.broadcast_to(convb_ref[:, cs], (ts, LRU_BLOCK))
            for k in range(CONV_WIDTH):
                off = b * hist + SUBLANES - (CONV_WIDTH - 1) + k
                acc = acc + xr_s[c, off:off + ts, :] * convw_ref[k:k + 1, cs]
            xc_s[c, b * ts:(b + 1) * ts, :] = acc
            xr_s[c, b * hist:b * hist + SUBLANES, :] = xr_s[c, b * hist + ts:(b + 1) * hist, :]

    lam = lam_ref[...]
    log_sig = jnp.minimum(lam, 0.0) - jnp.log1p(jnp.exp(-jnp.abs(lam)))
    ls_c = (LRU_C * math.log2(math.e)) * log_sig
    tpos = lax.broadcasted_iota(jnp.int32, (rows, 1), 0) & (ts - 1)
    seq_start = jnp.logical_and(tpos == 0, i == 0)
    for c in range(LRU_BLOCKS):
        cs = slice(c * LRU_BLOCK, (c + 1) * LRU_BLOCK)
        xc = xc_s[c]
        g = _dot(xc.astype(BF16), wg_ref[c]) + bg_ref[c]
        r = jax.nn.sigmoid(g[:, :LRU_BLOCK])
        ig = jax.nn.sigmoid(g[:, LRU_BLOCK:])
        a = jnp.exp2(r * ls_c[:, cs])
        t = 1.0 - a * a
        mult = jnp.where(t > 0.0, t * lax.rsqrt(t), 0.0)
        mult = jnp.where(seq_start, 1.0, mult)
        u = mult * (ig * xc)
        for b in range(BATCH):
            a_s[c, b * pitch:b * pitch + ts, :] = a[b * ts:(b + 1) * ts]
            u_s[c, b * pitch:b * pitch + ts, :] = u[b * ts:(b + 1) * ts]

        if c % 2 == 1:
            gs = slice((c - 1) * LRU_BLOCK, (c + 1) * LRU_BLOCK)
            gate = _gelu_tanh(_dot(hn, win_ref[:, LRU_WIDTH + gs.start:LRU_WIDTH + gs.stop]))
            xc_s[c - 1] = gate[:, :LRU_BLOCK]
            xc_s[c] = gate[:, LRU_BLOCK:]


    def mem_scores(b, h):
        slab = slice((h // 2) * LANES, (h // 2 + 1) * LANES)
        q = qt_s[h * HEAD_DIM:(h + 1) * HEAD_DIM, b * ts:(b + 1) * ts]
        sm_s[b % 2, h] = _dot(mkv_ref[b, :, slab], _pad_head_rows(q, h % 2))

    def mem_output(b, h):
        s = sm_s[b % 2, h]
        m = jnp.max(s, axis=0, keepdims=True)
        e = jnp.exp(s - m)
        den = jnp.sum(e, axis=0, keepdims=True)
        ot = _dot(mvt_ref[b, h * HEAD_DIM:(h + 1) * HEAD_DIM, :], e.astype(BF16)) * (1.0 / den)
        mt_s[h * HEAD_DIM:(h + 1) * HEAD_DIM, b * ts:(b + 1) * ts] = ot

    h = [h_s[:, c * LRU_BLOCK:(c + 1) * LRU_BLOCK] for c in range(LRU_BLOCKS)]
    steps_per_chunk = ts // BATCH
    for hd in range(MEM_HEADS):
        mem_scores(0, hd)
    for b in range(BATCH):
        for t in range(b * steps_per_chunk, (b + 1) * steps_per_chunk):
            idx = pl.ds(t, BATCH, stride=pitch)
            for c in range(LRU_BLOCKS):
                h[c] = a_s[c, idx, :] * h[c] + u_s[c, idx, :]
                u_s[c, idx, :] = h[c]
        for hd in range(MEM_HEADS):
            if b + 1 < BATCH:
                mem_scores(b + 1, hd)
            mem_output(b, hd)
    for c in range(LRU_BLOCKS):
        h_s[:, c * LRU_BLOCK:(c + 1) * LRU_BLOCK] = h[c]

    nb = BATCH // OUT_CHUNKS_A
    for oc in range(OUT_CHUNKS_A):
        for b in range(oc * nb, (oc + 1) * nb):
            rs = slice(b * ts, (b + 1) * ts)
            for c in range(LRU_BLOCKS):
                cs = slice(c * LRU_BLOCK, (c + 1) * LRU_BLOCK)
                cat_s[rs, cs] = (u_s[c, b * pitch:b * pitch + ts, :] * xc_s[c, rs, :]).astype(BF16)
            cat_s[rs, LRU_WIDTH:] = mt_s[:, rs].T.astype(BF16)
        rows_c = slice(oc * nb * ts, (oc + 1) * nb * ts)
        mixed = _dot(cat_s[rows_c, :], wout_ref[...])
        x_c = x_ref[oc * nb:(oc + 1) * nb].reshape(nb * ts, D_MODEL)
        o_ref[oc * nb:(oc + 1) * nb] = (x_c + _rms(mixed, gpost_ref[...])).reshape(nb, ts, D_MODEL)


def _mixer_a(x, gpre, w_in, w_q_t, conv_w, conv_b, w_gate, b_gate, lam, mkv, mvt, w_out_all, gpost):
    ts, pitch = TS_A, PITCH_A
    rows = BATCH * ts
    return pl.pallas_call(
        _mixer_a_kernel,
        grid=(SEQ // ts,),
        in_specs=[
            pl.BlockSpec((BATCH, ts, D_MODEL), lambda i: (0, i, 0)),
            _const_spec((1, D_MODEL)),
            _const_spec(w_in.shape),
            _const_spec(w_q_t.shape),
            _const_spec(conv_w.shape),
            _const_spec((1, LRU_WIDTH)),
            _const_spec(w_gate.shape),
            _const_spec(b_gate.shape),
            _const_spec((1, LRU_WIDTH)),
            _layer_spec(mkv.shape, 0),
            _layer_spec(mvt.shape, 0),
            _layer_spec(w_out_all.shape, 0),
            _const_spec((1, D_MODEL)),
        ],
        out_specs=pl.BlockSpec((BATCH, ts, D_MODEL), lambda i: (0, i, 0)),
        out_shape=jax.ShapeDtypeStruct((BATCH, SEQ, D_MODEL), F32),
        scratch_shapes=[
            pltpu.VMEM((LRU_BLOCKS, BATCH * (ts + SUBLANES), LRU_BLOCK), F32),
            pltpu.VMEM((LRU_BLOCKS, rows, LRU_BLOCK), F32),
            pltpu.VMEM((LRU_BLOCKS, BATCH * pitch, LRU_BLOCK), F32),
            pltpu.VMEM((LRU_BLOCKS, BATCH * pitch, LRU_BLOCK), F32),
            pltpu.VMEM((MEM_WIDTH, rows), BF16),
            pltpu.VMEM((2, MEM_HEADS, N_MEM, ts), F32),
            pltpu.VMEM((MEM_WIDTH, rows), F32),
            pltpu.VMEM((rows, D_MODEL), BF16),
            pltpu.VMEM((BATCH, LRU_WIDTH), F32),
        ],
        compiler_params=pltpu.CompilerParams(dimension_semantics=("arbitrary",),
                                             vmem_limit_bytes=VMEM_LIMIT),
        name="mixer_a",
    )(x, gpre.reshape(1, D_MODEL), w_in, w_q_t, conv_w, conv_b.reshape(1, LRU_WIDTH), w_gate, b_gate,
      lam.reshape(1, LRU_WIDTH), mkv, mvt, w_out_all, gpost.reshape(1, D_MODEL))


def _ffn_body(h_ref, gpre_ref, win_ref, wout_ref, gpost_ref):
    h = h_ref[...]
    hn = _rms(h, gpre_ref[...]).astype(BF16)
    acc = jnp.zeros((h.shape[0], D_MODEL), F32)
    for ck in range(D_FF // FF_CHUNK):
        lo = ck * FF_CHUNK
        g = _dot(hn, win_ref[:, lo:lo + FF_CHUNK].astype(BF16))
        u = _dot(hn, win_ref[:, D_FF + lo:D_FF + lo + FF_CHUNK].astype(BF16))
        act = (g * jax.nn.sigmoid(g) * u).astype(BF16)
        acc = acc + _dot(act, wout_ref[lo:lo + FF_CHUNK, :].astype(BF16))
    return h + _rms(acc, gpost_ref[...])


def _ffn_kernel(h_ref, gpre_ref, win_ref, wout_ref, gpost_ref, o_ref):
    o_ref[...] = _ffn_body(h_ref, gpre_ref, win_ref, wout_ref, gpost_ref)


def _ffn_kv_kernel(h_ref, gpre_ref, win_ref, wout_ref, gpost_ref, gkv_ref, wk_ref, wvt_ref, cos_ref, sin_ref,
                   o_ref, k_ref, vt_ref):
    h_new = _ffn_body(h_ref, gpre_ref, win_ref, wout_ref, gpost_ref)
    o_ref[...] = h_new
    hn = _rms(h_new, gkv_ref[...]).astype(BF16)
    k = _dot(hn, wk_ref[...])
    cos, sin = cos_ref[...], sin_ref[...]
    for j in range(KV_WIDTH // LANES):
        k_ref[:, j * LANES:(j + 1) * LANES] = _rope(k[:, j * LANES:(j + 1) * LANES], cos, sin).astype(BF16)
    vt_ref[0] = _dot_nt(wvt_ref[...], hn).astype(BF16)


def _ffn(h2d, layer, gpre, w_in, w_out, gpost, kv_args=None):
    tokens = h2d.shape[0]
    tm = TM_FFN if kv_args is None else TM_FFN_KV
    row_spec = pl.BlockSpec((tm, D_MODEL), lambda i: (i, 0))
    in_specs = [row_spec, _const_spec((1, D_MODEL)), _layer_spec(w_in.shape, layer),
                _layer_spec(w_out.shape, layer), _const_spec((1, D_MODEL))]
    args = [h2d, gpre.reshape(1, D_MODEL), w_in, w_out, gpost.reshape(1, D_MODEL)]
    params = pltpu.CompilerParams(dimension_semantics=("arbitrary",),
                                  vmem_limit_bytes=VMEM_LIMIT if kv_args is None else VMEM_LIMIT_FFN_KV)
    if kv_args is None:
        return pl.pallas_call(
            _ffn_kernel, grid=(tokens // tm,), in_specs=in_specs, out_specs=row_spec,
            out_shape=jax.ShapeDtypeStruct((tokens, D_MODEL), F32),
            compiler_params=params, name="ffn",
        )(*args)
    gkv, wk, wvt, cos, sin = kv_args
    pos_blocks = SEQ // tm
    table_spec = pl.BlockSpec((tm, LANES), lambda i: (i % pos_blocks, 0))
    k_spec = pl.BlockSpec((tm, KV_WIDTH), lambda i: (i, 0))
    vt_spec = pl.BlockSpec((1, KV_WIDTH, tm), lambda i: (i // pos_blocks, 0, i % pos_blocks))
    return pl.pallas_call(
        _ffn_kv_kernel, grid=(tokens // tm,),
        in_specs=in_specs + [_const_spec((1, D_MODEL)), _const_spec(wk.shape), _const_spec(wvt.shape),
                             table_spec, table_spec],
        out_specs=[row_spec, k_spec, vt_spec],
        out_shape=[jax.ShapeDtypeStruct((tokens, D_MODEL), F32),
                   jax.ShapeDtypeStruct((tokens, KV_WIDTH), BF16),
                   jax.ShapeDtypeStruct((BATCH, KV_WIDTH, SEQ), BF16)],
        compiler_params=params, name="ffn_kv",
    )(*args, gkv.reshape(1, D_MODEL), wk, wvt, cos, sin)


def _mixer_b_kernel(sinks_ref, x_ref, gpre_ref, wint_ref, cost_ref, sint_ref, k_ref, vt_ref, mkv_ref, mvt_ref,
                    wout_ref, gpost_ref, o_ref, qt_s, s_s, sm_s, catt_s, bound_s, e_s):
    b = pl.program_id(0)
    j = pl.program_id(1)
    tq = TQ_B
    n_qb = tq // WINDOW
    q_width = SWA_Q_HEADS * HEAD_DIM
    grp_rows = SWA_GROUP * HEAD_DIM
    tc = tq // PROJ_CHUNKS_B
    for c in range(PROJ_CHUNKS_B):
        ts_ = slice(c * tc, (c + 1) * tc)
        hn = _rms(x_ref[0, ts_, :], gpre_ref[...]).astype(BF16)
        cos_t, sin_t = cost_ref[:, ts_], sint_ref[:, ts_]
        for kh in range(SWA_KV_HEADS):
            pt = _dot_nt(wint_ref[kh * grp_rows:(kh + 1) * grp_rows, :], hn)
            for g in range(SWA_GROUP):
                r0 = g * HEAD_DIM
                x1, x2 = pt[r0:r0 + ROPE_HALF], pt[r0 + ROPE_HALF:r0 + ROPE_DIM]
                head = jnp.concatenate([x1 * cos_t - x2 * sin_t, x2 * cos_t + x1 * sin_t,
                                        pt[r0 + ROPE_DIM:r0 + HEAD_DIM]], axis=0)
                qt_s[kh * grp_rows + r0:kh * grp_rows + r0 + HEAD_DIM, ts_] = (head * ATTN_SCALE).astype(BF16)
        qt_s[q_width:, ts_] = (_dot_nt(wint_ref[q_width:, :], hn) * ATTN_SCALE).astype(BF16)

    @pl.when(jnp.logical_and(b == 0, j == 0))
    def _():
        ci = lax.broadcasted_iota(jnp.int32, (2 * WINDOW, WINDOW), 0)
        qi = lax.broadcasted_iota(jnp.int32, (2 * WINDOW, WINDOW), 1)
        dist = ci - qi
        band = jnp.logical_and(dist > 0, dist <= WINDOW)
        bound_s[0] = jnp.where(band, jnp.inf, NEG_INF)
        bound_s[1] = jnp.where(jnp.logical_and(band, ci >= WINDOW), jnp.inf, NEG_INF)

    def starts(qb):
        start = pl.multiple_of(j * tq + qb * WINDOW, WINDOW)
        return pl.multiple_of(jnp.maximum(start - WINDOW, 0), WINDOW), start

    def swa_scores(qb, kh):
        pstart, start = starts(qb)
        slab = slice((kh // 2) * LANES, (kh // 2 + 1) * LANES)
        band = jnp.concatenate([k_ref[0, pl.ds(pstart, WINDOW), slab], k_ref[0, pl.ds(start, WINDOW), slab]],
                               axis=0)
        qs = slice(qb * WINDOW, (qb + 1) * WINDOW)
        qcat = jnp.concatenate([qt_s[(kh * SWA_GROUP + g) * HEAD_DIM:(kh * SWA_GROUP + g + 1) * HEAD_DIM, qs]
                                for g in range(SWA_GROUP)], axis=1)
        s_s[qb % 2, kh] = _dot(band, _pad_head_rows(qcat, kh % 2))

    def swa_output(qb, kh):
        pstart, start = starts(qb)
        bound = bound_s[jnp.where(j > 0, 0, 1)] if qb == 0 else bound_s[0]
        slot = kh % 2
        rden = []
        for g in range(SWA_GROUP):
            gs = slice(g * WINDOW, (g + 1) * WINDOW)
            s = jnp.minimum(s_s[qb % 2, kh, :, gs], bound)
            sink = jnp.full((1, WINDOW), sinks_ref[kh * SWA_GROUP + g], F32)
            m = jnp.maximum(jnp.max(s, axis=0, keepdims=True), sink)
            e = jnp.exp(s - m)
            rden.append(1.0 / (jnp.sum(e, axis=0, keepdims=True) + jnp.exp(sink - m)))
            e_s[slot, :, gs] = e.astype(BF16)
        rows = slice(kh * HEAD_DIM, (kh + 1) * HEAD_DIM)
        vband = jnp.concatenate([vt_ref[0, rows, pl.ds(pstart, WINDOW)], vt_ref[0, rows, pl.ds(start, WINDOW)]],
                                axis=1)
        ot = _dot(vband, e_s[slot, :, 0:SWA_GROUP * WINDOW])
        for g in range(SWA_GROUP):
            h = kh * SWA_GROUP + g
            catt_s[h * HEAD_DIM:(h + 1) * HEAD_DIM, qb * WINDOW:(qb + 1) * WINDOW] = (
                ot[:, g * WINDOW:(g + 1) * WINDOW] * rden[g]).astype(BF16)

    def mem_scores(h):
        slab = slice((h // 2) * LANES, (h // 2 + 1) * LANES)
        q = qt_s[q_width + h * HEAD_DIM:q_width + (h + 1) * HEAD_DIM, :]
        sm_s[h] = _dot(mkv_ref[b, :, slab], _pad_head_rows(q, h % 2))

    def mem_output(h):
        slot = h % 2
        rden = []
        for g in range(tq // LANES):
            gs = slice(g * LANES, (g + 1) * LANES)
            s = sm_s[h, :, gs]
            e = jnp.exp(s - jnp.max(s, axis=0, keepdims=True))
            rden.append(1.0 / jnp.sum(e, axis=0, keepdims=True))
            e_s[slot, :, gs] = e.astype(BF16)
        ot = _dot(mvt_ref[b, h * HEAD_DIM:(h + 1) * HEAD_DIM, :], e_s[slot]) * jnp.concatenate(rden, axis=1)
        catt_s[q_width + h * HEAD_DIM:q_width + (h + 1) * HEAD_DIM, :] = ot.astype(BF16)

    for kh in range(SWA_KV_HEADS):
        swa_scores(0, kh)
    for qb in range(n_qb):
        for kh in range(SWA_KV_HEADS):
            if qb + 1 < n_qb:
                swa_scores(qb + 1, kh)
            else:
                mem_scores(kh)
            swa_output(qb, kh)
    for h in range(MEM_HEADS):
        mem_output(h)

    for c in range(PROJ_CHUNKS_B):
        ts_ = slice(c * tc, (c + 1) * tc)
        mixed = lax.dot_general(catt_s[:, ts_], wout_ref[...], (((0,), (0,)), ((), ())),
                                preferred_element_type=F32)
        o_ref[0, ts_, :] = x_ref[0, ts_, :] + _rms(mixed, gpost_ref[...])


def _mixer_b(h, sinks, gpre, w_in_t, cos_t, sin_t, k, vt, mkv, mvt, w_out_all, gpost):
    tq = TQ_B
    grid_spec = pltpu.PrefetchScalarGridSpec(
        num_scalar_prefetch=1,
        grid=(BATCH, SEQ // tq),
        in_specs=[
            pl.BlockSpec((1, tq, D_MODEL), lambda b, j, s: (b, j, 0)),
            _const_spec((1, D_MODEL)),
            _const_spec(w_in_t.shape),
            pl.BlockSpec((ROPE_HALF, tq), lambda b, j, s: (0, j)),
            pl.BlockSpec((ROPE_HALF, tq), lambda b, j, s: (0, j)),
            pl.BlockSpec((1, SEQ, KV_WIDTH), lambda b, j, s: (b, 0, 0)),
            pl.BlockSpec((1, KV_WIDTH, SEQ), lambda b, j, s: (b, 0, 0)),
            _layer_spec(mkv.shape, 1),
            _layer_spec(mvt.shape, 1),
            _layer_spec(w_out_all.shape, 1),
            _const_spec((1, D_MODEL)),
        ],
        out_specs=pl.BlockSpec((1, tq, D_MODEL), lambda b, j, s: (b, j, 0)),
        scratch_shapes=[
            pltpu.VMEM((D_MODEL, tq), BF16),
            pltpu.VMEM((2, SWA_KV_HEADS, 2 * WINDOW, SWA_GROUP * WINDOW), F32),
            pltpu.VMEM((MEM_HEADS, N_MEM, tq), F32),
            pltpu.VMEM((D_MODEL, tq), BF16),
            pltpu.VMEM((2, 2 * WINDOW, WINDOW), F32),
            pltpu.VMEM((2, N_MEM, tq), BF16),
        ],
    )
    return pl.pallas_call(
        _mixer_b_kernel,
        grid_spec=grid_spec,
        out_shape=jax.ShapeDtypeStruct((BATCH, SEQ, D_MODEL), F32),
        compiler_params=pltpu.CompilerParams(dimension_semantics=("arbitrary", "arbitrary"),
                                             vmem_limit_bytes=VMEM_LIMIT),
        name="mixer_b",
    )(sinks, h, gpre.reshape(1, D_MODEL), w_in_t, cos_t, sin_t, k, vt, mkv, mvt, w_out_all,
      gpost.reshape(1, D_MODEL))


def _rope_tables():
    inv_freq = 1.0 / (ROPE_THETA ** (jnp.arange(0, ROPE_DIM, 2, dtype=F32) / ROPE_DIM))
    ang = jnp.arange(SEQ, dtype=F32)[:, None] * inv_freq[None, :]
    c, s = jnp.cos(ang), jnp.sin(ang)
    rest = HEAD_DIM - ROPE_DIM
    cos_h = jnp.concatenate([c, c, jnp.ones((SEQ, rest), F32)], axis=-1)
    sin_h = jnp.concatenate([-s, s, jnp.zeros((SEQ, rest), F32)], axis=-1)
    reps = LANES // HEAD_DIM
    return jnp.tile(cos_h, (1, reps)), jnp.tile(sin_h, (1, reps)), c.T, s.T


def kernel(x, mem, norm_mix_pre, norm_mix_post, norm_ffn_pre, norm_ffn_post, mem_norm, w_mem_kv, w_in_a,
           conv_w, conv_b, w_gate_r, b_gate_r, w_gate_i, b_gate_i, lru_lambda, norm_kv, w_kv_shared, w_in_b,
           sinks, w_out, w_ffn_in, w_ffn_out):
    tokens = BATCH * SEQ
    cos, sin, cos_t, sin_t = _rope_tables()
    w_gate = jnp.concatenate([w_gate_r[0], w_gate_i[0]], axis=-1).astype(BF16)
    b_gate = jnp.concatenate([b_gate_r[0].reshape(LRU_BLOCKS, 1, LRU_BLOCK),
                              b_gate_i[0].reshape(LRU_BLOCKS, 1, LRU_BLOCK)], axis=-1)

    w_out_bf = w_out.astype(BF16)
    w_mem_vt = jnp.swapaxes(w_mem_kv[:, :, MEM_WIDTH:], 1, 2).astype(BF16)
    mkv, mvt = _memkv(mem, mem_norm, w_mem_kv.astype(BF16), w_mem_vt)

    h = _mixer_a(x, norm_mix_pre[0], w_in_a[0, :, :2 * LRU_WIDTH].astype(BF16),
                 w_in_a[0, :, 2 * LRU_WIDTH:].T.astype(BF16), conv_w[0], conv_b[0], w_gate, b_gate,
                 lru_lambda[0], mkv, mvt, w_out_bf, norm_mix_post[0])
    h2d, k, vt = _ffn(h.reshape(tokens, D_MODEL), 0, norm_ffn_pre[0], w_ffn_in, w_ffn_out, norm_ffn_post[0],
                      kv_args=(norm_kv, w_kv_shared[:, :KV_WIDTH].astype(BF16),
                               w_kv_shared[:, KV_WIDTH:].T.astype(BF16), cos, sin))

    h = _mixer_b(h2d.reshape(BATCH, SEQ, D_MODEL), sinks[0], norm_mix_pre[1], w_in_b[0].T.astype(BF16),
                 cos_t, sin_t, k.reshape(BATCH, SEQ, KV_WIDTH), vt, mkv, mvt, w_out_bf, norm_mix_post[1])
    h2d = _ffn(h.reshape(tokens, D_MODEL), 1, norm_ffn_pre[1], w_ffn_in, w_ffn_out, norm_ffn_post[1])
    return h2d.reshape(BATCH, SEQ, D_MODEL)
```

```python
import functools
import math

import jax
import jax.numpy as jnp
from jax import lax
from jax.experimental import pallas as pl
from jax.experimental.pallas import tpu as pltpu

D_MODEL = 1024
BATCH = 8
SEQ = 4096
HEAD_DIM = 64
MEM_HEADS = 4
MEM_WIDTH = MEM_HEADS * HEAD_DIM
N_MEM = 256
LRU_WIDTH = D_MODEL - MEM_WIDTH
LRU_BLOCKS = 6
LRU_BLOCK = LRU_WIDTH // LRU_BLOCKS
CONV_WIDTH = 4
LRU_C = 8.0
SWA_Q_HEADS = 12
SWA_KV_HEADS = 4
SWA_GROUP = SWA_Q_HEADS // SWA_KV_HEADS
KV_WIDTH = SWA_KV_HEADS * HEAD_DIM
WINDOW = 128
ROPE_DIM = HEAD_DIM // 4
ROPE_HALF = ROPE_DIM // 2
ROPE_THETA = 500000.0
D_FF = 2816
EPS = 1e-6
NEG_INF = -1e30
ATTN_SCALE = HEAD_DIM ** -0.5

LANES = 128
SUBLANES = 8
VMEM_LIMIT = 56 * 1024 * 1024
VMEM_LIMIT_FFN_KV = 62 * 1024 * 1024

TS_A = 128
PITCH_A = TS_A + SUBLANES
TM_FFN = 1024
TM_FFN_KV = 1024
FF_CHUNK = 256
TQ_B = 512
PROJ_CHUNKS_B = 2
OUT_CHUNKS_A = 4

BF16 = jnp.bfloat16
F32 = jnp.float32


def _rms(x, g):
    ms = jnp.mean(x * x, axis=-1, keepdims=True)
    return x * lax.rsqrt(ms + EPS) * g


def _dot(a, b):
    return jnp.dot(a, b, preferred_element_type=F32)


def _dot_nt(a, b):
    return lax.dot_general(a, b, (((1,), (1,)), ((), ())), preferred_element_type=F32)


def _gelu_tanh(x):
    k1 = -2.0 * math.sqrt(2.0 / math.pi) * math.log2(math.e)
    return x / (1.0 + jnp.exp2(x * (k1 + (k1 * 0.044715) * (x * x))))


def _rope(x, cos, sin):
    lane = lax.broadcasted_iota(jnp.int32, x.shape, 1) & (HEAD_DIM - 1)
    partner = jnp.where(lane < ROPE_HALF,
                        pltpu.roll(x, LANES - ROPE_HALF, 1),
                        pltpu.roll(x, ROPE_HALF, 1))
    return x * cos + partner * sin


def _memkv_kernel(mem_ref, g_ref, w_ref, wvt_ref, o_ref, ovt_ref):
    mem_n = _rms(mem_ref[0], g_ref[...]).astype(BF16)
    for l in range(2):
        o_ref[l, 0] = _dot(mem_n, w_ref[l]).astype(BF16)
        ovt_ref[l, 0] = _dot_nt(wvt_ref[l], mem_n).astype(BF16)


def _memkv(mem, mem_norm, w_mem_kv, w_mem_vt):
    return pl.pallas_call(
        _memkv_kernel,
        grid=(BATCH,),
        in_specs=[
            pl.BlockSpec((1, N_MEM, D_MODEL), lambda b: (b, 0, 0)),
            pl.BlockSpec((1, D_MODEL), lambda b: (0, 0)),
            pl.BlockSpec((2, D_MODEL, 2 * MEM_WIDTH), lambda b: (0, 0, 0)),
            pl.BlockSpec((2, MEM_WIDTH, D_MODEL), lambda b: (0, 0, 0)),
        ],
        out_specs=[pl.BlockSpec((2, 1, N_MEM, 2 * MEM_WIDTH), lambda b: (0, b, 0, 0)),
                   pl.BlockSpec((2, 1, MEM_WIDTH, N_MEM), lambda b: (0, b, 0, 0))],
        out_shape=[jax.ShapeDtypeStruct((2, BATCH, N_MEM, 2 * MEM_WIDTH), BF16),
                   jax.ShapeDtypeStruct((2, BATCH, MEM_WIDTH, N_MEM), BF16)],
        compiler_params=pltpu.CompilerParams(dimension_semantics=("arbitrary",),
                                             vmem_limit_bytes=VMEM_LIMIT),
        name="memkv",
    )(mem, mem_norm.reshape(1, D_MODEL), w_mem_kv, w_mem_vt)


def _const_spec(shape):
    nd = len(shape)
    return pl.BlockSpec(shape, lambda *_: (0,) * nd, pipeline_mode=pl.Buffered(1))


def _layer_spec(stacked_shape, layer):
    nd = len(stacked_shape) - 1
    return pl.BlockSpec((None,) + tuple(stacked_shape[1:]), lambda *_: (layer,) + (0,) * nd,
                        pipeline_mode=pl.Buffered(1))


def _pad_head_rows(q, half):
    z = jnp.zeros_like(q)
    return jnp.concatenate([z, q] if half else [q, z], axis=0)


def _mixer_a_kernel(x_ref, gpre_ref, win_ref, wqt_ref, convw_ref, convb_ref, wg_ref, bg_ref, lam_ref,
                    mkv_ref, mvt_ref, wout_ref, gpost_ref, o_ref,
                    xr_s, xc_s, a_s, u_s, qt_s, sm_s, mt_s, cat_s, h_s):
    i = pl.program_id(0)
    ts, pitch = TS_A, PITCH_A
    rows = BATCH * ts

    @pl.when(i == 0)
    def _():
        h_s[...] = jnp.zeros_like(h_s)
        xr_s[...] = jnp.zeros_like(xr_s)

    hist = ts + SUBLANES
    nb = BATCH // OUT_CHUNKS_A
    hn_parts = []
    for rc in range(OUT_CHUNKS_A):
        x_c = x_ref[rc * nb:(rc + 1) * nb].reshape(nb * ts, D_MODEL)
        hn_c = _rms(x_c, gpre_ref[...]).astype(BF16)
        hn_parts.append(hn_c)
        xr = _dot(hn_c, win_ref[:, 0:LRU_WIDTH])
        for c in range(LRU_BLOCKS):
            cs = slice(c * LRU_BLOCK, (c + 1) * LRU_BLOCK)
            for bb in range(nb):
                b = rc * nb + bb
                xr_s[c, b * hist + SUBLANES:(b + 1) * hist, :] = xr[bb * ts:(bb + 1) * ts, cs]
    hn = jnp.concatenate(hn_parts, axis=0)

    qt_s[...] = (_dot_nt(wqt_ref[...], hn) * ATTN_SCALE).astype(BF16)

    for c in range(LRU_BLOCKS):
        cs = slice(c * LRU_BLOCK, (c + 1) * LRU_BLOCK)
        for b in range(BATCH):
            acc = jnp.broadcast_to(convb_ref[:, cs], (ts, LRU_BLOCK))
            for k in range(CONV_WIDTH):
                off = b * hist + SUBLANES - (CONV_WIDTH - 1) + k
                acc = acc + xr_s[c, off:off + ts, :] * convw_ref[k:k + 1, cs]
            xc_s[c, b * ts:(b + 1) * ts, :] = acc
            xr_s[c, b * hist:b * hist + SUBLANES, :] = xr_s[c, b * hist + ts:(b + 1) * hist, :]

    lam = lam_ref[...]
    log_sig = jnp.minimum(lam, 0.0) - jnp.log1p(jnp.exp(-jnp.abs(lam)))
    ls_c = (LRU_C * math.log2(math.e)) * log_sig
    tpos = lax.broadcasted_iota(jnp.int32, (rows, 1), 0) & (ts - 1)
    seq_start = jnp.logical_and(tpos == 0, i == 0)
    for c in range(LRU_BLOCKS):
        cs = slice(c * LRU_BLOCK, (c + 1) * LRU_BLOCK)
        xc = xc_s[c]
        g = _dot(xc.astype(BF16), wg_ref[c]) + bg_ref[c]
        r = jax.nn.sigmoid(g[:, :LRU_BLOCK])
        ig = jax.nn.sigmoid(g[:, LRU_BLOCK:])
        a = jnp.exp2(r * ls_c[:, cs])
        t = 1.0 - a * a
        mult = jnp.where(t > 0.0, t * lax.rsqrt(t), 0.0)
        mult = jnp.where(seq_start, 1.0, mult)
        u = mult * (ig * xc)
        for b in range(BATCH):
            a_s[c, b * pitch:b * pitch + ts, :] = a[b * ts:(b + 1) * ts]
            u_s[c, b * pitch:b * pitch + ts, :] = u[b * ts:(b + 1) * ts]

        if c % 2 == 1:
            gs = slice((c - 1) * LRU_BLOCK, (c + 1) * LRU_BLOCK)
            gate = _gelu_tanh(_dot(hn, win_ref[:, LRU_WIDTH + gs.start:LRU_WIDTH + gs.stop]))
            xc_s[c - 1] = gate[:, :LRU_BLOCK]
            xc_s[c] = gate[:, LRU_BLOCK:]


    def mem_scores(b, h):
        slab = slice((h // 2) * LANES, (h // 2 + 1) * LANES)
        q = qt_s[h * HEAD_DIM:(h + 1) * HEAD_DIM, b * ts:(b + 1) * ts]
        sm_s[b % 2, h] = _dot(mkv_ref[b, :, slab], _pad_head_rows(q, h % 2))

    def mem_output(b, h):
        s = sm_s[b % 2, h]
        m = jnp.max(s, axis=0, keepdims=True)
        e = jnp.exp(s - m)
        den = jnp.sum(e, axis=0, keepdims=True)
        ot = _dot(mvt_ref[b, h * HEAD_DIM:(h + 1) * HEAD_DIM, :], e.astype(BF16)) * (1.0 / den)
        mt_s[h * HEAD_DIM:(h + 1) * HEAD_DIM, b * ts:(b + 1) * ts] = ot

    h = [h_s[:, c * LRU_BLOCK:(c + 1) * LRU_BLOCK] for c in range(LRU_BLOCKS)]
    steps_per_chunk = ts // BATCH
    for hd in range(MEM_HEADS):
        mem_scores(0, hd)
    for b in range(BATCH):
        for t in range(b * steps_per_chunk, (b + 1) * steps_per_chunk):
            idx = pl.ds(t, BATCH, stride=pitch)
            for c in range(LRU_BLOCKS):
                h[c] = a_s[c, idx, :] * h[c] + u_s[c, idx, :]
                u_s[c, idx, :] = h[c]
        for hd in range(MEM_HEADS):
            if b + 1 < BATCH:
                mem_scores(b + 1, hd)
            mem_output(b, hd)
    for c in range(LRU_BLOCKS):
        h_s[:, c * LRU_BLOCK:(c + 1) * LRU_BLOCK] = h[c]

    nb = BATCH // OUT_CHUNKS_A
    for oc in range(OUT_CHUNKS_A):
        for b in range(oc * nb, (oc + 1) * nb):
            rs = slice(b * ts, (b + 1) * ts)
            for c in range(LRU_BLOCKS):
                cs = slice(c * LRU_BLOCK, (c + 1) * LRU_BLOCK)
                cat_s[rs, cs] = (u_s[c, b * pitch:b * pitch + ts, :] * xc_s[c, rs, :]).astype(BF16)
            cat_s[rs, LRU_WIDTH:] = mt_s[:, rs].T.astype(BF16)
        rows_c = slice(oc * nb * ts, (oc + 1) * nb * ts)
        mixed = _dot(cat_s[rows_c, :], wout_ref[...])
        x_c = x_ref[oc * nb:(oc + 1) * nb].reshape(nb * ts, D_MODEL)
        o_ref[oc * nb:(oc + 1) * nb] = (x_c + _rms(mixed, gpost_ref[...])).reshape(nb, ts, D_MODEL)


def _mixer_a(x, gpre, w_in, w_q_t, conv_w, conv_b, w_gate, b_gate, lam, mkv, mvt, w_out_all, gpost):
    ts, pitch = TS_A, PITCH_A
    rows = BATCH * ts
    return pl.pallas_call(
        _mixer_a_kernel,
        grid=(SEQ // ts,),
        in_specs=[
            pl.BlockSpec((BATCH, ts, D_MODEL), lambda i: (0, i, 0)),
            _const_spec((1, D_MODEL)),
            _const_spec(w_in.shape),
            _const_spec(w_q_t.shape),
            _const_spec(conv_w.shape),
            _const_spec((1, LRU_WIDTH)),
            _const_spec(w_gate.shape),
            _const_spec(b_gate.shape),
            _const_spec((1, LRU_WIDTH)),
            _layer_spec(mkv.shape, 0),
            _layer_spec(mvt.shape, 0),
            _layer_spec(w_out_all.shape, 0),
            _const_spec((1, D_MODEL)),
        ],
        out_specs=pl.BlockSpec((BATCH, ts, D_MODEL), lambda i: (0, i, 0)),
        out_shape=jax.ShapeDtypeStruct((BATCH, SEQ, D_MODEL), F32),
        scratch_shapes=[
            pltpu.VMEM((LRU_BLOCKS, BATCH * (ts + SUBLANES), LRU_BLOCK), F32),
            pltpu.VMEM((LRU_BLOCKS, rows, LRU_BLOCK), F32),
            pltpu.VMEM((LRU_BLOCKS, BATCH * pitch, LRU_BLOCK), F32),
            pltpu.VMEM((LRU_BLOCKS, BATCH * pitch, LRU_BLOCK), F32),
            pltpu.VMEM((MEM_WIDTH, rows), BF16),
            pltpu.VMEM((2, MEM_HEADS, N_MEM, ts), F32),
            pltpu.VMEM((MEM_WIDTH, rows), F32),
            pltpu.VMEM((rows, D_MODEL), BF16),
            pltpu.VMEM((BATCH, LRU_WIDTH), F32),
        ],
        compiler_params=pltpu.CompilerParams(dimension_semantics=("arbitrary",),
                                             vmem_limit_bytes=VMEM_LIMIT),
        name="mixer_a",
    )(x, gpre.reshape(1, D_MODEL), w_in, w_q_t, conv_w, conv_b.reshape(1, LRU_WIDTH), w_gate, b_gate,
      lam.reshape(1, LRU_WIDTH), mkv, mvt, w_out_all, gpost.reshape(1, D_MODEL))


def _ffn_body(h_ref, gpre_ref, win_ref, wout_ref, gpost_ref):
    h = h_ref[...]
    hn = _rms(h, gpre_ref[...]).astype(BF16)
    acc = jnp.zeros((h.shape[0], D_MODEL), F32)
    for ck in range(D_FF // FF_CHUNK):
        lo = ck * FF_CHUNK
        g = _dot(hn, win_ref[:, lo:lo + FF_CHUNK].astype(BF16))
        u = _dot(hn, win_ref[:, D_FF + lo:D_FF + lo + FF_CHUNK].astype(BF16))
        act = (g * jax.nn.sigmoid(g) * u).astype(BF16)
        acc = acc + _dot(act, wout_ref[lo:lo + FF_CHUNK, :].astype(BF16))
    return h + _rms(acc, gpost_ref[...])


def _ffn_kernel(h_ref, gpre_ref, win_ref, wout_ref, gpost_ref, o_ref):
    o_ref[...] = _ffn_body(h_ref, gpre_ref, win_ref, wout_ref, gpost_ref)


def _ffn_kv_kernel(h_ref, gpre_ref, win_ref, wout_ref, gpost_ref, gkv_ref, wk_ref, wvt_ref, cos_ref, sin_ref,
                   o_ref, k_ref, vt_ref):
    h_new = _ffn_body(h_ref, gpre_ref, win_ref, wout_ref, gpost_ref)
    o_ref[...] = h_new
    hn = _rms(h_new, gkv_ref[...]).astype(BF16)
    k = _dot(hn, wk_ref[...])
    cos, sin = cos_ref[...], sin_ref[...]
    for j in range(KV_WIDTH // LANES):
        k_ref[:, j * LANES:(j + 1) * LANES] = _rope(k[:, j * LANES:(j + 1) * LANES], cos, sin).astype(BF16)
    vt_ref[0] = _dot_nt(wvt_ref[...], hn).astype(BF16)


def _ffn(h2d, layer, gpre, w_in, w_out, gpost, kv_args=None):
    tokens = h2d.shape[0]
    tm = TM_FFN if kv_args is None else TM_FFN_KV
    row_spec = pl.BlockSpec((tm, D_MODEL), lambda i: (i, 0))
    in_specs = [row_spec, _const_spec((1, D_MODEL)), _layer_spec(w_in.shape, layer),
                _layer_spec(w_out.shape, layer), _const_spec((1, D_MODEL))]
    args = [h2d, gpre.reshape(1, D_MODEL), w_in, w_out, gpost.reshape(1, D_MODEL)]
    params = pltpu.CompilerParams(dimension_semantics=("arbitrary",),
                                  vmem_limit_bytes=VMEM_LIMIT if kv_args is None else VMEM_LIMIT_FFN_KV)
    if kv_args is None:
        return pl.pallas_call(
            _ffn_kernel, grid=(tokens // tm,), in_specs=in_specs, out_specs=row_spec,
            out_shape=jax.ShapeDtypeStruct((tokens, D_MODEL), F32),
            compiler_params=params, name="ffn",
        )(*args)
    gkv, wk, wvt, cos, sin = kv_args
    pos_blocks = SEQ // tm
    table_spec = pl.BlockSpec((tm, LANES), lambda i: (i % pos_blocks, 0))
    k_spec = pl.BlockSpec((tm, KV_WIDTH), lambda i: (i, 0))
    vt_spec = pl.BlockSpec((1, KV_WIDTH, tm), lambda i: (i // pos_blocks, 0, i % pos_blocks))
    return pl.pallas_call(
        _ffn_kv_kernel, grid=(tokens // tm,),
        in_specs=in_specs + [_const_spec((1, D_MODEL)), _const_spec(wk.shape), _const_spec(wvt.shape),
                             table_spec, table_spec],
        out_specs=[row_spec, k_spec, vt_spec],
        out_shape=[jax.ShapeDtypeStruct((tokens, D_MODEL), F32),
                   jax.ShapeDtypeStruct((tokens, KV_WIDTH), BF16),
                   jax.ShapeDtypeStruct((BATCH, KV_WIDTH, SEQ), BF16)],
        compiler_params=params, name="ffn_kv",
    )(*args, gkv.reshape(1, D_MODEL), wk, wvt, cos, sin)


def _mixer_b_kernel(sinks_ref, x_ref, gpre_ref, wint_ref, cost_ref, sint_ref, k_ref, vt_ref, mkv_ref, mvt_ref,
                    wout_ref, gpost_ref, o_ref, qt_s, s_s, sm_s, catt_s, bound_s, e_s):
    b = pl.program_id(0)
    j = pl.program_id(1)
    tq = TQ_B
    n_qb = tq // WINDOW
    q_width = SWA_Q_HEADS * HEAD_DIM
    grp_rows = SWA_GROUP * HEAD_DIM
    tc = tq // PROJ_CHUNKS_B
    for c in range(PROJ_CHUNKS_B):
        ts_ = slice(c * tc, (c + 1) * tc)
        hn = _rms(x_ref[0, ts_, :], gpre_ref[...]).astype(BF16)
        cos_t, sin_t = cost_ref[:, ts_], sint_ref[:, ts_]
        for kh in range(SWA_KV_HEADS):
            pt = _dot_nt(wint_ref[kh * grp_rows:(kh + 1) * grp_rows, :], hn)
            for g in range(SWA_GROUP):
                r0 = g * HEAD_DIM
                x1, x2 = pt[r0:r0 + ROPE_HALF], pt[r0 + ROPE_HALF:r0 + ROPE_DIM]
                head = jnp.concatenate([x1 * cos_t - x2 * sin_t, x2 * cos_t + x1 * sin_t,
                                        pt[r0 + ROPE_DIM:r0 + HEAD_DIM]], axis=0)
                qt_s[kh * grp_rows + r0:kh * grp_rows + r0 + HEAD_DIM, ts_] = (head * ATTN_SCALE).astype(BF16)
        qt_s[q_width:, ts_] = (_dot_nt(wint_ref[q_width:, :], hn) * ATTN_SCALE).astype(BF16)

    @pl.when(jnp.logical_and(b == 0, j == 0))
    def _():
        ci = lax.broadcasted_iota(jnp.int32, (2 * WINDOW, WINDOW), 0)
        qi = lax.broadcasted_iota(jnp.int32, (2 * WINDOW, WINDOW), 1)
        dist = ci - qi
        band = jnp.logical_and(dist > 0, dist <= WINDOW)
        bound_s[0] = jnp.where(band, jnp.inf, NEG_INF)
        bound_s[1] = jnp.where(jnp.logical_and(band, ci >= WINDOW), jnp.inf, NEG_INF)

    def starts(qb):
        start = pl.multiple_of(j * tq + qb * WINDOW, WINDOW)
        return pl.multiple_of(jnp.maximum(start - WINDOW, 0), WINDOW), start

    def swa_scores(qb, kh):
        pstart, start = starts(qb)
        slab = slice((kh // 2) * LANES, (kh // 2 + 1) * LANES)
        band = jnp.concatenate([k_ref[0, pl.ds(pstart, WINDOW), slab], k_ref[0, pl.ds(start, WINDOW), slab]],
                               axis=0)
        qs = slice(qb * WINDOW, (qb + 1) * WINDOW)
        qcat = jnp.concatenate([qt_s[(kh * SWA_GROUP + g) * HEAD_DIM:(kh * SWA_GROUP + g + 1) * HEAD_DIM, qs]
                                for g in range(SWA_GROUP)], axis=1)
        s_s[qb % 2, kh] = _dot(band, _pad_head_rows(qcat, kh % 2))

    def swa_output(qb, kh):
        pstart, start = starts(qb)
        bound = bound_s[jnp.where(j > 0, 0, 1)] if qb == 0 else bound_s[0]
        slot = kh % 2
        rden = []
        for g in range(SWA_GROUP):
            gs = slice(g * WINDOW, (g + 1) * WINDOW)
            s = jnp.minimum(s_s[qb % 2, kh, :, gs], bound)
            sink = jnp.full((1, WINDOW), sinks_ref[kh * SWA_GROUP + g], F32)
            m = jnp.maximum(jnp.max(s, axis=0, keepdims=True), sink)
            e = jnp.exp(s - m)
            rden.append(1.0 / (jnp.sum(e, axis=0, keepdims=True) + jnp.exp(sink - m)))
            e_s[slot, :, gs] = e.astype(BF16)
        rows = slice(kh * HEAD_DIM, (kh + 1) * HEAD_DIM)
        vband = jnp.concatenate([vt_ref[0, rows, pl.ds(pstart, WINDOW)], vt_ref[0, rows, pl.ds(start, WINDOW)]],
                                axis=1)
        ot = _dot(vband, e_s[slot, :, 0:SWA_GROUP * WINDOW])
        for g in range(SWA_GROUP):
            h = kh * SWA_GROUP + g
            catt_s[h * HEAD_DIM:(h + 1) * HEAD_DIM, qb * WINDOW:(qb + 1) * WINDOW] = (
                ot[:, g * WINDOW:(g + 1) * WINDOW] * rden[g]).astype(BF16)

    def mem_scores(h):
        slab = slice((h // 2) * LANES, (h // 2 + 1) * LANES)
        q = qt_s[q_width + h * HEAD_DIM:q_width + (h + 1) * HEAD_DIM, :]
        sm_s[h] = _dot(mkv_ref[b, :, slab], _pad_head_rows(q, h % 2))

    def mem_output(h):
        slot = h % 2
        rden = []
        for g in range(tq // LANES):
            gs = slice(g * LANES, (g + 1) * LANES)
            s = sm_s[h, :, gs]
            e = jnp.exp(s - jnp.max(s, axis=0, keepdims=True))
            rden.append(1.0 / jnp.sum(e, axis=0, keepdims=True))
            e_s[slot, :, gs] = e.astype(BF16)
        ot = _dot(mvt_ref[b, h * HEAD_DIM:(h + 1) * HEAD_DIM, :], e_s[slot]) * jnp.concatenate(rden, axis=1)
        catt_s[q_width + h * HEAD_DIM:q_width + (h + 1) * HEAD_DIM, :] = ot.astype(BF16)

    for kh in range(SWA_KV_HEADS):
        swa_scores(0, kh)
    for qb in range(n_qb):
        for kh in range(SWA_KV_HEADS):
            if qb + 1 < n_qb:
                swa_scores(qb + 1, kh)
            else:
                mem_scores(kh)
            swa_output(qb, kh)
    for h in range(MEM_HEADS):
        mem_output(h)

    for c in range(PROJ_CHUNKS_B):
        ts_ = slice(c * tc, (c + 1) * tc)
        mixed = lax.dot_general(catt_s[:, ts_], wout_ref[...], (((0,), (0,)), ((), ())),
                                preferred_element_type=F32)
        o_ref[0, ts_, :] = x_ref[0, ts_, :] + _rms(mixed, gpost_ref[...])


def _mixer_b(h, sinks, gpre, w_in_t, cos_t, sin_t, k, vt, mkv, mvt, w_out_all, gpost):
    tq = TQ_B
    grid_spec = pltpu.PrefetchScalarGridSpec(
        num_scalar_prefetch=1,
        grid=(BATCH, SEQ // tq),
        in_specs=[
            pl.BlockSpec((1, tq, D_MODEL), lambda b, j, s: (b, j, 0)),
            _const_spec((1, D_MODEL)),
            _const_spec(w_in_t.shape),
            pl.BlockSpec((ROPE_HALF, tq), lambda b, j, s: (0, j)),
            pl.BlockSpec((ROPE_HALF, tq), lambda b, j, s: (0, j)),
            pl.BlockSpec((1, SEQ, KV_WIDTH), lambda b, j, s: (b, 0, 0)),
            pl.BlockSpec((1, KV_WIDTH, SEQ), lambda b, j, s: (b, 0, 0)),
            _layer_spec(mkv.shape, 1),
            _layer_spec(mvt.shape, 1),
            _layer_spec(w_out_all.shape, 1),
            _const_spec((1, D_MODEL)),
        ],
        out_specs=pl.BlockSpec((1, tq, D_MODEL), lambda b, j, s: (b, j, 0)),
        scratch_shapes=[
            pltpu.VMEM((D_MODEL, tq), BF16),
            pltpu.VMEM((2, SWA_KV_HEADS, 2 * WINDOW, SWA_GROUP * WINDOW), F32),
            pltpu.VMEM((MEM_HEADS, N_MEM, tq), F32),
            pltpu.VMEM((D_MODEL, tq), BF16),
            pltpu.VMEM((2, 2 * WINDOW, WINDOW), F32),
            pltpu.VMEM((2, N_MEM, tq), BF16),
        ],
    )
    return pl.pallas_call(
        _mixer_b_kernel,
        grid_spec=grid_spec,
        out_shape=jax.ShapeDtypeStruct((BATCH, SEQ, D_MODEL), F32),
        compiler_params=pltpu.CompilerParams(dimension_semantics=("arbitrary", "arbitrary"),
                                             vmem_limit_bytes=VMEM_LIMIT),
        name="mixer_b",
    )(sinks, h, gpre.reshape(1, D_MODEL), w_in_t, cos_t, sin_t, k, vt, mkv, mvt, w_out_all,
      gpost.reshape(1, D_MODEL))


def _rope_tables():
    inv_freq = 1.0 / (ROPE_THETA ** (jnp.arange(0, ROPE_DIM, 2, dtype=F32) / ROPE_DIM))
    ang = jnp.arange(SEQ, dtype=F32)[:, None] * inv_freq[None, :]
    c, s = jnp.cos(ang), jnp.sin(ang)
    rest = HEAD_DIM - ROPE_DIM
    cos_h = jnp.concatenate([c, c, jnp.ones((SEQ, rest), F32)], axis=-1)
    sin_h = jnp.concatenate([-s, s, jnp.zeros((SEQ, rest), F32)], axis=-1)
    reps = LANES // HEAD_DIM
    return jnp.tile(cos_h, (1, reps)), jnp.tile(sin_h, (1, reps)), c.T, s.T


def kernel(x, mem, norm_mix_pre, norm_mix_post, norm_ffn_pre, norm_ffn_post, mem_norm, w_mem_kv, w_in_a,
           conv_w, conv_b, w_gate_r, b_gate_r, w_gate_i, b_gate_i, lru_lambda, norm_kv, w_kv_shared, w_in_b,
           sinks, w_out, w_ffn_in, w_ffn_out):
    tokens = BATCH * SEQ
    cos, sin, cos_t, sin_t = _rope_tables()
    w_gate = jnp.concatenate([w_gate_r[0], w_gate_i[0]], axis=-1).astype(BF16)
    b_gate = jnp.concatenate([b_gate_r[0].reshape(LRU_BLOCKS, 1, LRU_BLOCK),
                              b_gate_i[0].reshape(LRU_BLOCKS, 1, LRU_BLOCK)], axis=-1)

    w_out_bf = w_out.astype(BF16)
    w_mem_vt = jnp.swapaxes(w_mem_kv[:, :, MEM_WIDTH:], 1, 2).astype(BF16)
    mkv, mvt = _memkv(mem, mem_norm, w_mem_kv.astype(BF16), w_mem_vt)

    h = _mixer_a(x, norm_mix_pre[0], w_in_a[0, :, :2 * LRU_WIDTH].astype(BF16),
                 w_in_a[0, :, 2 * LRU_WIDTH:].T.astype(BF16), conv_w[0], conv_b[0], w_gate, b_gate,
                 lru_lambda[0], mkv, mvt, w_out_bf, norm_mix_post[0])
    h2d, k, vt = _ffn(h.reshape(tokens, D_MODEL), 0, norm_ffn_pre[0], w_ffn_in, w_ffn_out, norm_ffn_post[0],
                      kv_args=(norm_kv, w_kv_shared[:, :KV_WIDTH].astype(BF16),
                               w_kv_shared[:, KV_WIDTH:].T.astype(BF16), cos, sin))

    h = _mixer_b(h2d.reshape(BATCH, SEQ, D_MODEL), sinks[0], norm_mix_pre[1], w_in_b[0].T.astype(BF16),
                 cos_t, sin_t, k.reshape(BATCH, SEQ, KV_WIDTH), vt, mkv, mvt, w_out_bf, norm_mix_post[1])
    h2d = _ffn(h.reshape(tokens, D_MODEL), 1, norm_ffn_pre[1], w_ffn_in, w_ffn_out, norm_ffn_post[1])
    return h2d.reshape(BATCH, SEQ, D_MODEL)
```

```python
import functools
import math

import jax
import jax.numpy as jnp
from jax import lax
from jax.experimental import pallas as pl
from jax.experimental.pallas import tpu as pltpu

D_MODEL = 1024
BATCH = 8
SEQ = 4096
HEAD_DIM = 64
MEM_HEADS = 4
MEM_WIDTH = MEM_HEADS * HEAD_DIM
N_MEM = 256
LRU_WIDTH = D_MODEL - MEM_WIDTH
LRU_BLOCKS = 6
LRU_BLOCK = LRU_WIDTH // LRU_BLOCKS
CONV_WIDTH = 4
LRU_C = 8.0
SWA_Q_HEADS = 12
SWA_KV_HEADS = 4
SWA_GROUP = SWA_Q_HEADS // SWA_KV_HEADS
KV_WIDTH = SWA_KV_HEADS * HEAD_DIM
WINDOW = 128
ROPE_DIM = HEAD_DIM // 4
ROPE_HALF = ROPE_DIM // 2
ROPE_THETA = 500000.0
D_FF = 2816
EPS = 1e-6
NEG_INF = -1e30
ATTN_SCALE = HEAD_DIM ** -0.5

LANES = 128
SUBLANES = 8
VMEM_LIMIT = 56 * 1024 * 1024
VMEM_LIMIT_FFN_KV = 62 * 1024 * 1024

TS_A = 128
PITCH_A = TS_A + 4
TM_FFN = 1024
TM_FFN_KV = 1024
FF_CHUNK = 256
TQ_B = 512
PROJ_CHUNKS_B = 2
OUT_CHUNKS_A = 4

BF16 = jnp.bfloat16
F32 = jnp.float32


def _rms(x, g):
    ms = jnp.mean(x * x, axis=-1, keepdims=True)
    return x * lax.rsqrt(ms + EPS) * g


def _dot(a, b):
    return jnp.dot(a, b, preferred_element_type=F32)


def _dot_nt(a, b):
    return lax.dot_general(a, b, (((1,), (1,)), ((), ())), preferred_element_type=F32)


def _gelu_tanh(x):
    k1 = -2.0 * math.sqrt(2.0 / math.pi) * math.log2(math.e)
    return x / (1.0 + jnp.exp2(x * (k1 + (k1 * 0.044715) * (x * x))))


def _rope(x, cos, sin):
    lane = lax.broadcasted_iota(jnp.int32, x.shape, 1) & (HEAD_DIM - 1)
    partner = jnp.where(lane < ROPE_HALF,
                        pltpu.roll(x, LANES - ROPE_HALF, 1),
                        pltpu.roll(x, ROPE_HALF, 1))
    return x * cos + partner * sin


def _memkv_kernel(mem_ref, g_ref, w_ref, wvt_ref, o_ref, ovt_ref):
    mem_n = _rms(mem_ref[0], g_ref[...]).astype(BF16)
    for l in range(2):
        o_ref[l, 0] = _dot(mem_n, w_ref[l]).astype(BF16)
        ovt_ref[l, 0] = _dot_nt(wvt_ref[l], mem_n).astype(BF16)


def _memkv(mem, mem_norm, w_mem_kv, w_mem_vt):
    return pl.pallas_call(
        _memkv_kernel,
        grid=(BATCH,),
        in_specs=[
            pl.BlockSpec((1, N_MEM, D_MODEL), lambda b: (b, 0, 0)),
            pl.BlockSpec((1, D_MODEL), lambda b: (0, 0)),
            pl.BlockSpec((2, D_MODEL, 2 * MEM_WIDTH), lambda b: (0, 0, 0)),
            pl.BlockSpec((2, MEM_WIDTH, D_MODEL), lambda b: (0, 0, 0)),
        ],
        out_specs=[pl.BlockSpec((2, 1, N_MEM, 2 * MEM_WIDTH), lambda b: (0, b, 0, 0)),
                   pl.BlockSpec((2, 1, MEM_WIDTH, N_MEM), lambda b: (0, b, 0, 0))],
        out_shape=[jax.ShapeDtypeStruct((2, BATCH, N_MEM, 2 * MEM_WIDTH), BF16),
                   jax.ShapeDtypeStruct((2, BATCH, MEM_WIDTH, N_MEM), BF16)],
        compiler_params=pltpu.CompilerParams(dimension_semantics=("arbitrary",),
                                             vmem_limit_bytes=VMEM_LIMIT),
        name="memkv",
    )(mem, mem_norm.reshape(1, D_MODEL), w_mem_kv, w_mem_vt)


def _const_spec(shape):
    nd = len(shape)
    return pl.BlockSpec(shape, lambda *_: (0,) * nd, pipeline_mode=pl.Buffered(1))


def _layer_spec(stacked_shape, layer):
    nd = len(stacked_shape) - 1
    return pl.BlockSpec((None,) + tuple(stacked_shape[1:]), lambda *_: (layer,) + (0,) * nd,
                        pipeline_mode=pl.Buffered(1))


def _pad_head_rows(q, half):
    z = jnp.zeros_like(q)
    return jnp.concatenate([z, q] if half else [q, z], axis=0)


def _mixer_a_kernel(x_ref, gpre_ref, win_ref, wqt_ref, convw_ref, convb_ref, wg_ref, bg_ref, lam_ref,
                    mkv_ref, mvt_ref, wout_ref, gpost_ref, o_ref,
                    xr_s, xc_s, a_s, u_s, qt_s, sm_s, mt_s, cat_s, h_s):
    i = pl.program_id(0)
    ts, pitch = TS_A, PITCH_A
    rows = BATCH * ts

    @pl.when(i == 0)
    def _():
        h_s[...] = jnp.zeros_like(h_s)
        xr_s[...] = jnp.zeros_like(xr_s)

    hist = ts + SUBLANES
    nb = BATCH // OUT_CHUNKS_A
    hn_parts = []
    for rc in range(OUT_CHUNKS_A):
        x_c = x_ref[rc * nb:(rc + 1) * nb].reshape(nb * ts, D_MODEL)
        hn_c = _rms(x_c, gpre_ref[...]).astype(BF16)
        hn_parts.append(hn_c)
        xr = _dot(hn_c, win_ref[:, 0:LRU_WIDTH])
        for c in range(LRU_BLOCKS):
            cs = slice(c * LRU_BLOCK, (c + 1) * LRU_BLOCK)
            for bb in range(nb):
                b = rc * nb + bb
                xr_s[c, b * hist + SUBLANES:(b + 1) * hist, :] = xr[bb * ts:(bb + 1) * ts, cs]
    hn = jnp.concatenate(hn_parts, axis=0)

    qt_s[...] = (_dot_nt(wqt_ref[...], hn) * ATTN_SCALE).astype(BF16)

    for c in range(LRU_BLOCKS):
        cs = slice(c * LRU_BLOCK, (c + 1) * LRU_BLOCK)
        for b in range(BATCH):
            acc = jnp.broadcast_to(convb_ref[:, cs], (ts, LRU_BLOCK))
            for k in range(CONV_WIDTH):
                off = b * hist + SUBLANES - (CONV_WIDTH - 1) + k
                acc = acc + xr_s[c, off:off + ts, :] * convw_ref[k:k + 1, cs]
            xc_s[c, b * ts:(b + 1) * ts, :] = acc
            xr_s[c, b * hist:b * hist + SUBLANES, :] = xr_s[c, b * hist + ts:(b + 1) * hist, :]

    lam = lam_ref[...]
    log_sig = jnp.minimum(lam, 0.0) - jnp.log1p(jnp.exp(-jnp.abs(lam)))
    ls_c = (LRU_C * math.log2(math.e)) * log_sig
    tpos = lax.broadcasted_iota(jnp.int32, (rows, 1), 0) & (ts - 1)
    seq_start = jnp.logical_and(tpos == 0, i == 0)
    for c in range(LRU_BLOCKS):
        cs = slice(c * LRU_BLOCK, (c + 1) * LRU_BLOCK)
        xc = xc_s[c]
        g = _dot(xc.astype(BF16), wg_ref[c]) + bg_ref[c]
        r = jax.nn.sigmoid(g[:, :LRU_BLOCK])
        ig = jax.nn.sigmoid(g[:, LRU_BLOCK:])
        a = jnp.exp2(r * ls_c[:, cs])
        t = 1.0 - a * a
        mult = jnp.where(t > 0.0, t * lax.rsqrt(t), 0.0)
        mult = jnp.where(seq_start, 1.0, mult)
        u = mult * (ig * xc)
        for b in range(BATCH):
            a_s[c, b * pitch:b * pitch + ts, :] = a[b * ts:(b + 1) * ts]
            u_s[c, b * pitch:b * pitch + ts, :] = u[b * ts:(b + 1) * ts]

        if c % 2 == 1:
            gs = slice((c - 1) * LRU_BLOCK, (c + 1) * LRU_BLOCK)
            gate = _gelu_tanh(_dot(hn, win_ref[:, LRU_WIDTH + gs.start:LRU_WIDTH + gs.stop]))
            xc_s[c - 1] = gate[:, :LRU_BLOCK]
            xc_s[c] = gate[:, LRU_BLOCK:]


    def mem_scores(b, h):
        slab = slice((h // 2) * LANES, (h // 2 + 1) * LANES)
        q = qt_s[h * HEAD_DIM:(h + 1) * HEAD_DIM, b * ts:(b + 1) * ts]
        sm_s[b % 2, h] = _dot(mkv_ref[b, :, slab], _pad_head_rows(q, h % 2))

    def mem_output(b, h):
        s = sm_s[b % 2, h]
        m = jnp.max(s, axis=0, keepdims=True)
        e = jnp.exp(s - m)
        den = jnp.sum(e, axis=0, keepdims=True)
        ot = _dot(mvt_ref[b, h * HEAD_DIM:(h + 1) * HEAD_DIM, :], e.astype(BF16)) * (1.0 / den)
        mt_s[h * HEAD_DIM:(h + 1) * HEAD_DIM, b * ts:(b + 1) * ts] = ot

    h = [h_s[:, c * LRU_BLOCK:(c + 1) * LRU_BLOCK] for c in range(LRU_BLOCKS)]
    steps_per_chunk = ts // BATCH
    for hd in range(MEM_HEADS):
        mem_scores(0, hd)
    for b in range(BATCH):
        for t in range(b * steps_per_chunk, (b + 1) * steps_per_chunk):
            idx = pl.ds(t, BATCH, stride=pitch)
            for c in range(LRU_BLOCKS):
                h[c] = a_s[c, idx, :] * h[c] + u_s[c, idx, :]
                u_s[c, idx, :] = h[c]
        for hd in range(MEM_HEADS):
            if b + 1 < BATCH:
                mem_scores(b + 1, hd)
            mem_output(b, hd)
    for c in range(LRU_BLOCKS):
        h_s[:, c * LRU_BLOCK:(c + 1) * LRU_BLOCK] = h[c]

    nb = BATCH // OUT_CHUNKS_A
    for oc in range(OUT_CHUNKS_A):
        for b in range(oc * nb, (oc + 1) * nb):
            rs = slice(b * ts, (b + 1) * ts)
            for c in range(LRU_BLOCKS):
                cs = slice(c * LRU_BLOCK, (c + 1) * LRU_BLOCK)
                cat_s[rs, cs] = (u_s[c, b * pitch:b * pitch + ts, :] * xc_s[c, rs, :]).astype(BF16)
            cat_s[rs, LRU_WIDTH:] = mt_s[:, rs].T.astype(BF16)
        rows_c = slice(oc * nb * ts, (oc + 1) * nb * ts)
        mixed = _dot(cat_s[rows_c, :], wout_ref[...])
        x_c = x_ref[oc * nb:(oc + 1) * nb].reshape(nb * ts, D_MODEL)
        o_ref[oc * nb:(oc + 1) * nb] = (x_c + _rms(mixed, gpost_ref[...])).reshape(nb, ts, D_MODEL)


def _mixer_a(x, gpre, w_in, w_q_t, conv_w, conv_b, w_gate, b_gate, lam, mkv, mvt, w_out_all, gpost):
    ts, pitch = TS_A, PITCH_A
    rows = BATCH * ts
    return pl.pallas_call(
        _mixer_a_kernel,
        grid=(SEQ // ts,),
        in_specs=[
            pl.BlockSpec((BATCH, ts, D_MODEL), lambda i: (0, i, 0)),
            _const_spec((1, D_MODEL)),
            _const_spec(w_in.shape),
            _const_spec(w_q_t.shape),
            _const_spec(conv_w.shape),
            _const_spec((1, LRU_WIDTH)),
            _const_spec(w_gate.shape),
            _const_spec(b_gate.shape),
            _const_spec((1, LRU_WIDTH)),
            _layer_spec(mkv.shape, 0),
            _layer_spec(mvt.shape, 0),
            _layer_spec(w_out_all.shape, 0),
            _const_spec((1, D_MODEL)),
        ],
        out_specs=pl.BlockSpec((BATCH, ts, D_MODEL), lambda i: (0, i, 0)),
        out_shape=jax.ShapeDtypeStruct((BATCH, SEQ, D_MODEL), F32),
        scratch_shapes=[
            pltpu.VMEM((LRU_BLOCKS, BATCH * (ts + SUBLANES), LRU_BLOCK), F32),
            pltpu.VMEM((LRU_BLOCKS, rows, LRU_BLOCK), F32),
            pltpu.VMEM((LRU_BLOCKS, BATCH * pitch, LRU_BLOCK), F32),
            pltpu.VMEM((LRU_BLOCKS, BATCH * pitch, LRU_BLOCK), F32),
            pltpu.VMEM((MEM_WIDTH, rows), BF16),
            pltpu.VMEM((2, MEM_HEADS, N_MEM, ts), F32),
            pltpu.VMEM((MEM_WIDTH, rows), F32),
            pltpu.VMEM((rows, D_MODEL), BF16),
            pltpu.VMEM((BATCH, LRU_WIDTH), F32),
        ],
        compiler_params=pltpu.CompilerParams(dimension_semantics=("arbitrary",),
                                             vmem_limit_bytes=VMEM_LIMIT),
        name="mixer_a",
    )(x, gpre.reshape(1, D_MODEL), w_in, w_q_t, conv_w, conv_b.reshape(1, LRU_WIDTH), w_gate, b_gate,
      lam.reshape(1, LRU_WIDTH), mkv, mvt, w_out_all, gpost.reshape(1, D_MODEL))


def _ffn_body(h_ref, gpre_ref, win_ref, wout_ref, gpost_ref):
    h = h_ref[...]
    hn = _rms(h, gpre_ref[...]).astype(BF16)
    acc = jnp.zeros((h.shape[0], D_MODEL), F32)
    for ck in range(D_FF // FF_CHUNK):
        lo = ck * FF_CHUNK
        g = _dot(hn, win_ref[:, lo:lo + FF_CHUNK].astype(BF16))
        u = _dot(hn, win_ref[:, D_FF + lo:D_FF + lo + FF_CHUNK].astype(BF16))
        act = (g * jax.nn.sigmoid(g) * u).astype(BF16)
        acc = acc + _dot(act, wout_ref[lo:lo + FF_CHUNK, :].astype(BF16))
    return h + _rms(acc, gpost_ref[...])


def _ffn_kernel(h_ref, gpre_ref, win_ref, wout_ref, gpost_ref, o_ref):
    o_ref[...] = _ffn_body(h_ref, gpre_ref, win_ref, wout_ref, gpost_ref)


def _ffn_kv_kernel(h_ref, gpre_ref, win_ref, wout_ref, gpost_ref, gkv_ref, wk_ref, wvt_ref, cos_ref, sin_ref,
                   o_ref, k_ref, vt_ref):
    h_new = _ffn_body(h_ref, gpre_ref, win_ref, wout_ref, gpost_ref)
    o_ref[...] = h_new
    hn = _rms(h_new, gkv_ref[...]).astype(BF16)
    k = _dot(hn, wk_ref[...])
    cos, sin = cos_ref[...], sin_ref[...]
    for j in range(KV_WIDTH // LANES):
        k_ref[:, j * LANES:(j + 1) * LANES] = _rope(k[:, j * LANES:(j + 1) * LANES], cos, sin).astype(BF16)
    vt_ref[0] = _dot_nt(wvt_ref[...], hn).astype(BF16)


def _ffn(h2d, layer, gpre, w_in, w_out, gpost, kv_args=None):
    tokens = h2d.shape[0]
    tm = TM_FFN if kv_args is None else TM_FFN_KV
    row_spec = pl.BlockSpec((tm, D_MODEL), lambda i: (i, 0))
    in_specs = [row_spec, _const_spec((1, D_MODEL)), _layer_spec(w_in.shape, layer),
                _layer_spec(w_out.shape, layer), _const_spec((1, D_MODEL))]
    args = [h2d, gpre.reshape(1, D_MODEL), w_in, w_out, gpost.reshape(1, D_MODEL)]
    params = pltpu.CompilerParams(dimension_semantics=("arbitrary",),
                                  vmem_limit_bytes=VMEM_LIMIT if kv_args is None else VMEM_LIMIT_FFN_KV)
    if kv_args is None:
        return pl.pallas_call(
            _ffn_kernel, grid=(tokens // tm,), in_specs=in_specs, out_specs=row_spec,
            out_shape=jax.ShapeDtypeStruct((tokens, D_MODEL), F32),
            compiler_params=params, name="ffn",
        )(*args)
    gkv, wk, wvt, cos, sin = kv_args
    pos_blocks = SEQ // tm
    table_spec = pl.BlockSpec((tm, LANES), lambda i: (i % pos_blocks, 0))
    k_spec = pl.BlockSpec((tm, KV_WIDTH), lambda i: (i, 0))
    vt_spec = pl.BlockSpec((1, KV_WIDTH, tm), lambda i: (i // pos_blocks, 0, i % pos_blocks))
    return pl.pallas_call(
        _ffn_kv_kernel, grid=(tokens // tm,),
        in_specs=in_specs + [_const_spec((1, D_MODEL)), _const_spec(wk.shape), _const_spec(wvt.shape),
                             table_spec, table_spec],
        out_specs=[row_spec, k_spec, vt_spec],
        out_shape=[jax.ShapeDtypeStruct((tokens, D_MODEL), F32),
                   jax.ShapeDtypeStruct((tokens, KV_WIDTH), BF16),
                   jax.ShapeDtypeStruct((BATCH, KV_WIDTH, SEQ), BF16)],
        compiler_params=params, name="ffn_kv",
    )(*args, gkv.reshape(1, D_MODEL), wk, wvt, cos, sin)


def _mixer_b_kernel(sinks_ref, x_ref, gpre_ref, wint_ref, cost_ref, sint_ref, k_ref, vt_ref, mkv_ref, mvt_ref,
                    wout_ref, gpost_ref, o_ref, qt_s, s_s, sm_s, catt_s, bound_s, e_s):
    b = pl.program_id(0)
    j = pl.program_id(1)
    tq = TQ_B
    n_qb = tq // WINDOW
    q_width = SWA_Q_HEADS * HEAD_DIM
    grp_rows = SWA_GROUP * HEAD_DIM
    tc = tq // PROJ_CHUNKS_B
    for c in range(PROJ_CHUNKS_B):
        ts_ = slice(c * tc, (c + 1) * tc)
        hn = _rms(x_ref[0, ts_, :], gpre_ref[...]).astype(BF16)
        cos_t, sin_t = cost_ref[:, ts_], sint_ref[:, ts_]
        for kh in range(SWA_KV_HEADS):
            pt = _dot_nt(wint_ref[kh * grp_rows:(kh + 1) * grp_rows, :], hn)
            for g in range(SWA_GROUP):
                r0 = g * HEAD_DIM
                x1, x2 = pt[r0:r0 + ROPE_HALF], pt[r0 + ROPE_HALF:r0 + ROPE_DIM]
                head = jnp.concatenate([x1 * cos_t - x2 * sin_t, x2 * cos_t + x1 * sin_t,
                                        pt[r0 + ROPE_DIM:r0 + HEAD_DIM]], axis=0)
                qt_s[kh * grp_rows + r0:kh * grp_rows + r0 + HEAD_DIM, ts_] = (head * ATTN_SCALE).astype(BF16)
        qt_s[q_width:, ts_] = (_dot_nt(wint_ref[q_width:, :], hn) * ATTN_SCALE).astype(BF16)

    @pl.when(jnp.logical_and(b == 0, j == 0))
    def _():
        ci = lax.broadcasted_iota(jnp.int32, (2 * WINDOW, WINDOW), 0)
        qi = lax.broadcasted_iota(jnp.int32, (2 * WINDOW, WINDOW), 1)
        dist = ci - qi
        band = jnp.logical_and(dist > 0, dist <= WINDOW)
        bound_s[0] = jnp.where(band, jnp.inf, NEG_INF)
        bound_s[1] = jnp.where(jnp.logical_and(band, ci >= WINDOW), jnp.inf, NEG_INF)

    def starts(qb):
        start = pl.multiple_of(j * tq + qb * WINDOW, WINDOW)
        return pl.multiple_of(jnp.maximum(start - WINDOW, 0), WINDOW), start

    def swa_scores(qb, kh):
        pstart, start = starts(qb)
        slab = slice((kh // 2) * LANES, (kh // 2 + 1) * LANES)
        band = jnp.concatenate([k_ref[0, pl.ds(pstart, WINDOW), slab], k_ref[0, pl.ds(start, WINDOW), slab]],
                               axis=0)
        qs = slice(qb * WINDOW, (qb + 1) * WINDOW)
        qcat = jnp.concatenate([qt_s[(kh * SWA_GROUP + g) * HEAD_DIM:(kh * SWA_GROUP + g + 1) * HEAD_DIM, qs]
                                for g in range(SWA_GROUP)], axis=1)
        s_s[qb % 2, kh] = _dot(band, _pad_head_rows(qcat, kh % 2))

    def swa_output(qb, kh):
        pstart, start = starts(qb)
        bound = bound_s[jnp.where(j > 0, 0, 1)] if qb == 0 else bound_s[0]
        slot = kh % 2
        rden = []
        for g in range(SWA_GROUP):
            gs = slice(g * WINDOW, (g + 1) * WINDOW)
            s = jnp.minimum(s_s[qb % 2, kh, :, gs], bound)
            sink = jnp.full((1, WINDOW), sinks_ref[kh * SWA_GROUP + g], F32)
            m = jnp.maximum(jnp.max(s, axis=0, keepdims=True), sink)
            e = jnp.exp(s - m)
            rden.append(1.0 / (jnp.sum(e, axis=0, keepdims=True) + jnp.exp(sink - m)))
            e_s[slot, :, gs] = e.astype(BF16)
        rows = slice(kh * HEAD_DIM, (kh + 1) * HEAD_DIM)
        vband = jnp.concatenate([vt_ref[0, rows, pl.ds(pstart, WINDOW)], vt_ref[0, rows, pl.ds(start, WINDOW)]],
                                axis=1)
        ot = _dot(vband, e_s[slot, :, 0:SWA_GROUP * WINDOW])
        for g in range(SWA_GROUP):
            h = kh * SWA_GROUP + g
            catt_s[h * HEAD_DIM:(h + 1) * HEAD_DIM, qb * WINDOW:(qb + 1) * WINDOW] = (
                ot[:, g * WINDOW:(g + 1) * WINDOW] * rden[g]).astype(BF16)

    def mem_scores(h):
        slab = slice((h // 2) * LANES, (h // 2 + 1) * LANES)
        q = qt_s[q_width + h * HEAD_DIM:q_width + (h + 1) * HEAD_DIM, :]
        sm_s[h] = _dot(mkv_ref[b, :, slab], _pad_head_rows(q, h % 2))

    def mem_output(h):
        slot = h % 2
        rden = []
        for g in range(tq // LANES):
            gs = slice(g * LANES, (g + 1) * LANES)
            s = sm_s[h, :, gs]
            e = jnp.exp(s - jnp.max(s, axis=0, keepdims=True))
            rden.append(1.0 / jnp.sum(e, axis=0, keepdims=True))
            e_s[slot, :, gs] = e.astype(BF16)
        ot = _dot(mvt_ref[b, h * HEAD_DIM:(h + 1) * HEAD_DIM, :], e_s[slot]) * jnp.concatenate(rden, axis=1)
        catt_s[q_width + h * HEAD_DIM:q_width + (h + 1) * HEAD_DIM, :] = ot.astype(BF16)

    for kh in range(SWA_KV_HEADS):
        swa_scores(0, kh)
    for qb in range(n_qb):
        for kh in range(SWA_KV_HEADS):
            if qb + 1 < n_qb:
                swa_scores(qb + 1, kh)
            else:
                mem_scores(kh)
            swa_output(qb, kh)
    for h in range(MEM_HEADS):
        mem_output(h)

    for c in range(PROJ_CHUNKS_B):
        ts_ = slice(c * tc, (c + 1) * tc)
        mixed = lax.dot_general(catt_s[:, ts_], wout_ref[...], (((0,), (0,)), ((), ())),
                                preferred_element_type=F32)
        o_ref[0, ts_, :] = x_ref[0, ts_, :] + _rms(mixed, gpost_ref[...])


def _mixer_b(h, sinks, gpre, w_in_t, cos_t, sin_t, k, vt, mkv, mvt, w_out_all, gpost):
    tq = TQ_B
    grid_spec = pltpu.PrefetchScalarGridSpec(
        num_scalar_prefetch=1,
        grid=(BATCH, SEQ // tq),
        in_specs=[
            pl.BlockSpec((1, tq, D_MODEL), lambda b, j, s: (b, j, 0)),
            _const_spec((1, D_MODEL)),
            _const_spec(w_in_t.shape),
            pl.BlockSpec((ROPE_HALF, tq), lambda b, j, s: (0, j)),
            pl.BlockSpec((ROPE_HALF, tq), lambda b, j, s: (0, j)),
            pl.BlockSpec((1, SEQ, KV_WIDTH), lambda b, j, s: (b, 0, 0)),
            pl.BlockSpec((1, KV_WIDTH, SEQ), lambda b, j, s: (b, 0, 0)),
            _layer_spec(mkv.shape, 1),
            _layer_spec(mvt.shape, 1),
            _layer_spec(w_out_all.shape, 1),
            _const_spec((1, D_MODEL)),
        ],
        out_specs=pl.BlockSpec((1, tq, D_MODEL), lambda b, j, s: (b, j, 0)),
        scratch_shapes=[
            pltpu.VMEM((D_MODEL, tq), BF16),
            pltpu.VMEM((2, SWA_KV_HEADS, 2 * WINDOW, SWA_GROUP * WINDOW), F32),
            pltpu.VMEM((MEM_HEADS, N_MEM, tq), F32),
            pltpu.VMEM((D_MODEL, tq), BF16),
            pltpu.VMEM((2, 2 * WINDOW, WINDOW), F32),
            pltpu.VMEM((2, N_MEM, tq), BF16),
        ],
    )
    return pl.pallas_call(
        _mixer_b_kernel,
        grid_spec=grid_spec,
        out_shape=jax.ShapeDtypeStruct((BATCH, SEQ, D_MODEL), F32),
        compiler_params=pltpu.CompilerParams(dimension_semantics=("arbitrary", "arbitrary"),
                                             vmem_limit_bytes=VMEM_LIMIT),
        name="mixer_b",
    )(sinks, h, gpre.reshape(1, D_MODEL), w_in_t, cos_t, sin_t, k, vt, mkv, mvt, w_out_all,
      gpost.reshape(1, D_MODEL))


def _rope_tables():
    inv_freq = 1.0 / (ROPE_THETA ** (jnp.arange(0, ROPE_DIM, 2, dtype=F32) / ROPE_DIM))
    ang = jnp.arange(SEQ, dtype=F32)[:, None] * inv_freq[None, :]
    c, s = jnp.cos(ang), jnp.sin(ang)
    rest = HEAD_DIM - ROPE_DIM
    cos_h = jnp.concatenate([c, c, jnp.ones((SEQ, rest), F32)], axis=-1)
    sin_h = jnp.concatenate([-s, s, jnp.zeros((SEQ, rest), F32)], axis=-1)
    reps = LANES // HEAD_DIM
    return jnp.tile(cos_h, (1, reps)), jnp.tile(sin_h, (1, reps)), c.T, s.T


def kernel(x, mem, norm_mix_pre, norm_mix_post, norm_ffn_pre, norm_ffn_post, mem_norm, w_mem_kv, w_in_a,
           conv_w, conv_b, w_gate_r, b_gate_r, w_gate_i, b_gate_i, lru_lambda, norm_kv, w_kv_shared, w_in_b,
           sinks, w_out, w_ffn_in, w_ffn_out):
    tokens = BATCH * SEQ
    cos, sin, cos_t, sin_t = _rope_tables()
    w_gate = jnp.concatenate([w_gate_r[0], w_gate_i[0]], axis=-1).astype(BF16)
    b_gate = jnp.concatenate([b_gate_r[0].reshape(LRU_BLOCKS, 1, LRU_BLOCK),
                              b_gate_i[0].reshape(LRU_BLOCKS, 1, LRU_BLOCK)], axis=-1)

    w_out_bf = w_out.astype(BF16)
    w_mem_vt = jnp.swapaxes(w_mem_kv[:, :, MEM_WIDTH:], 1, 2).astype(BF16)
    mkv, mvt = _memkv(mem, mem_norm, w_mem_kv.astype(BF16), w_mem_vt)

    h = _mixer_a(x, norm_mix_pre[0], w_in_a[0, :, :2 * LRU_WIDTH].astype(BF16),
                 w_in_a[0, :, 2 * LRU_WIDTH:].T.astype(BF16), conv_w[0], conv_b[0], w_gate, b_gate,
                 lru_lambda[0], mkv, mvt, w_out_bf, norm_mix_post[0])
    h2d, k, vt = _ffn(h.reshape(tokens, D_MODEL), 0, norm_ffn_pre[0], w_ffn_in, w_ffn_out, norm_ffn_post[0],
                      kv_args=(norm_kv, w_kv_shared[:, :KV_WIDTH].astype(BF16),
                               w_kv_shared[:, KV_WIDTH:].T.astype(BF16), cos, sin))

    h = _mixer_b(h2d.reshape(BATCH, SEQ, D_MODEL), sinks[0], norm_mix_pre[1], w_in_b[0].T.astype(BF16),
                 cos_t, sin_t, k.reshape(BATCH, SEQ, KV_WIDTH), vt, mkv, mvt, w_out_bf, norm_mix_post[1])
    h2d = _ffn(h.reshape(tokens, D_MODEL), 1, norm_ffn_pre[1], w_ffn_in, w_ffn_out, norm_ffn_post[1])
    return h2d.reshape(BATCH, SEQ, D_MODEL)
```

```python
import math

import jax
import jax.numpy as jnp
from jax import lax
from jax.experimental import pallas as pl
from jax.experimental.pallas import tpu as pltpu

D_MODEL = 1024
BATCH = 8
SEQ = 4096
HEAD_DIM = 64
MEM_HEADS = 4
MEM_WIDTH = MEM_HEADS * HEAD_DIM
N_MEM = 256
LRU_WIDTH = D_MODEL - MEM_WIDTH
LRU_BLOCKS = 6
LRU_BLOCK = LRU_WIDTH // LRU_BLOCKS
CONV_WIDTH = 4
LRU_C = 8.0
SWA_Q_HEADS = 12
SWA_KV_HEADS = 4
SWA_GROUP = SWA_Q_HEADS // SWA_KV_HEADS
KV_WIDTH = SWA_KV_HEADS * HEAD_DIM
WINDOW = 128
ROPE_DIM = HEAD_DIM // 4
ROPE_HALF = ROPE_DIM // 2
ROPE_THETA = 500000.0
D_FF = 2816
EPS = 1e-6
NEG_INF = -1e30
ATTN_SCALE = HEAD_DIM ** -0.5

LANES = 128
SUBLANES = 8
VMEM_LIMIT = 56 * 1024 * 1024
VMEM_LIMIT_FFN_KV = 62 * 1024 * 1024

TS_A = 128
PITCH_A = TS_A + 4
TM_FFN = 1024
TM_FFN_KV = 1024
FF_CHUNK = 256
TQ_B = 512
PROJ_CHUNKS_B = 2
OUT_CHUNKS_A = 4

BF16 = jnp.bfloat16
F32 = jnp.float32


def _rms(x, g):
    ms = jnp.mean(x * x, axis=-1, keepdims=True)
    return x * lax.rsqrt(ms + EPS) * g


def _dot(a, b):
    return jnp.dot(a, b, preferred_element_type=F32)


def _dot_nt(a, b):
    return lax.dot_general(a, b, (((1,), (1,)), ((), ())), preferred_element_type=F32)


def _gelu_tanh(x):
    k1 = -2.0 * math.sqrt(2.0 / math.pi) * math.log2(math.e)
    return x / (1.0 + jnp.exp2(x * (k1 + (k1 * 0.044715) * (x * x))))


def _rope(x, cos, sin):
    lane = lax.broadcasted_iota(jnp.int32, x.shape, 1) & (HEAD_DIM - 1)
    partner = jnp.where(lane < ROPE_HALF,
                        pltpu.roll(x, LANES - ROPE_HALF, 1),
                        pltpu.roll(x, ROPE_HALF, 1))
    return x * cos + partner * sin


def _memkv_kernel(mem_ref, g_ref, w_ref, wvt_ref, o_ref, ovt_ref):
    mem_n = _rms(mem_ref[0], g_ref[...]).astype(BF16)
    for l in range(2):
        o_ref[l, 0] = _dot(mem_n, w_ref[l]).astype(BF16)
        ovt_ref[l, 0] = _dot_nt(wvt_ref[l], mem_n).astype(BF16)


def _memkv(mem, mem_norm, w_mem_kv, w_mem_vt):
    return pl.pallas_call(
        _memkv_kernel,
        grid=(BATCH,),
        in_specs=[
            pl.BlockSpec((1, N_MEM, D_MODEL), lambda b: (b, 0, 0)),
            pl.BlockSpec((1, D_MODEL), lambda b: (0, 0)),
            pl.BlockSpec((2, D_MODEL, 2 * MEM_WIDTH), lambda b: (0, 0, 0)),
            pl.BlockSpec((2, MEM_WIDTH, D_MODEL), lambda b: (0, 0, 0)),
        ],
        out_specs=[pl.BlockSpec((2, 1, N_MEM, 2 * MEM_WIDTH), lambda b: (0, b, 0, 0)),
                   pl.BlockSpec((2, 1, MEM_WIDTH, N_MEM), lambda b: (0, b, 0, 0))],
        out_shape=[jax.ShapeDtypeStruct((2, BATCH, N_MEM, 2 * MEM_WIDTH), BF16),
                   jax.ShapeDtypeStruct((2, BATCH, MEM_WIDTH, N_MEM), BF16)],
        compiler_params=pltpu.CompilerParams(dimension_semantics=("arbitrary",),
                                             vmem_limit_bytes=VMEM_LIMIT),
        name="memkv",
    )(mem, mem_norm.reshape(1, D_MODEL), w_mem_kv, w_mem_vt)


def _const_spec(shape):
    nd = len(shape)
    return pl.BlockSpec(shape, lambda *_: (0,) * nd, pipeline_mode=pl.Buffered(1))


def _layer_spec(stacked_shape, layer):
    nd = len(stacked_shape) - 1
    return pl.BlockSpec((None,) + tuple(stacked_shape[1:]), lambda *_: (layer,) + (0,) * nd,
                        pipeline_mode=pl.Buffered(1))


def _pad_head_rows(q, half):
    z = jnp.zeros_like(q)
    return jnp.concatenate([z, q] if half else [q, z], axis=0)


def _mixer_a_kernel(x_ref, gpre_ref, win_ref, wq_ref, convw_ref, convb_ref, wg_ref, bg_ref, lam_ref,
                    mkv_ref, mvt_ref, wout_ref, gpost_ref, o_ref,
                    xr_s, xc_s, a_s, u_s, qt_s, sm_s, mt_s, cat_s, h_s):
    i = pl.program_id(0)
    ts, pitch = TS_A, PITCH_A
    rows = BATCH * ts

    @pl.when(i == 0)
    def _():
        h_s[...] = jnp.zeros_like(h_s)
        xr_s[...] = jnp.zeros_like(xr_s)

    hist = ts + SUBLANES
    nb = BATCH // OUT_CHUNKS_A
    hn_parts = []
    for rc in range(OUT_CHUNKS_A):
        x_c = x_ref[rc * nb:(rc + 1) * nb].reshape(nb * ts, D_MODEL)
        hn_c = _rms(x_c, gpre_ref[...]).astype(BF16)
        hn_parts.append(hn_c)
        xr = _dot(hn_c, win_ref[:, 0:LRU_WIDTH])
        for c in range(LRU_BLOCKS):
            cs = slice(c * LRU_BLOCK, (c + 1) * LRU_BLOCK)
            for bb in range(nb):
                b = rc * nb + bb
                xr_s[c, b * hist + SUBLANES:(b + 1) * hist, :] = xr[bb * ts:(bb + 1) * ts, cs]
    hn = jnp.concatenate(hn_parts, axis=0)

    qt_s[...] = (_dot(hn, wq_ref[...]) * ATTN_SCALE).T.astype(BF16)

    for c in range(LRU_BLOCKS):
        cs = slice(c * LRU_BLOCK, (c + 1) * LRU_BLOCK)
        for b in range(BATCH):
            acc = jnp.broadcast_to(convb_ref[:, cs], (ts, LRU_BLOCK))
            for k in range(CONV_WIDTH):
                off = b * hist + SUBLANES - (CONV_WIDTH - 1) + k
                acc = acc + xr_s[c, off:off + ts, :] * convw_ref[k:k + 1, cs]
            xc_s[c, b * ts:(b + 1) * ts, :] = acc
            xr_s[c, b * hist:b * hist + SUBLANES, :] = xr_s[c, b * hist + ts:(b + 1) * hist, :]

    lam = lam_ref[...]
    log_sig = jnp.minimum(lam, 0.0) - jnp.log1p(jnp.exp(-jnp.abs(lam)))
    ls_c = (LRU_C * math.log2(math.e)) * log_sig
    tpos = lax.broadcasted_iota(jnp.int32, (rows, 1), 0) & (ts - 1)
    seq_start = jnp.logical_and(tpos == 0, i == 0)
    for c in range(LRU_BLOCKS):
        cs = slice(c * LRU_BLOCK, (c + 1) * LRU_BLOCK)
        xc = xc_s[c]
        g = _dot(xc.astype(BF16), wg_ref[c]) + bg_ref[c]
        r = jax.nn.sigmoid(g[:, :LRU_BLOCK])
        ig = jax.nn.sigmoid(g[:, LRU_BLOCK:])
        a = jnp.exp2(r * ls_c[:, cs])
        t = 1.0 - a * a
        mult = jnp.where(t > 0.0, t * lax.rsqrt(t), 0.0)
        mult = jnp.where(seq_start, 1.0, mult)
        u = mult * (ig * xc)
        for b in range(BATCH):
            a_s[c, b * pitch:b * pitch + ts, :] = a[b * ts:(b + 1) * ts]
            u_s[c, b * pitch:b * pitch + ts, :] = u[b * ts:(b + 1) * ts]

        if c % 2 == 1:
            gs = slice((c - 1) * LRU_BLOCK, (c + 1) * LRU_BLOCK)
            gate = _gelu_tanh(_dot(hn, win_ref[:, LRU_WIDTH + gs.start:LRU_WIDTH + gs.stop]))
            xc_s[c - 1] = gate[:, :LRU_BLOCK]
            xc_s[c] = gate[:, LRU_BLOCK:]

    def mem_scores(b, h):
        slab = slice((h // 2) * LANES, (h // 2 + 1) * LANES)
        q = qt_s[h * HEAD_DIM:(h + 1) * HEAD_DIM, b * ts:(b + 1) * ts]
        sm_s[b % 2, h] = _dot(mkv_ref[b, :, slab], _pad_head_rows(q, h % 2))

    def mem_output(b, h):
        s = sm_s[b % 2, h]
        m = jnp.max(s, axis=0, keepdims=True)
        e = jnp.exp(s - m)
        den = jnp.sum(e, axis=0, keepdims=True)
        ot = _dot(mvt_ref[b, h * HEAD_DIM:(h + 1) * HEAD_DIM, :], e.astype(BF16)) * (1.0 / den)
        mt_s[h * HEAD_DIM:(h + 1) * HEAD_DIM, b * ts:(b + 1) * ts] = ot

    h = [h_s[:, c * LRU_BLOCK:(c + 1) * LRU_BLOCK] for c in range(LRU_BLOCKS)]
    steps_per_chunk = ts // BATCH
    for hd in range(MEM_HEADS):
        mem_scores(0, hd)
    for b in range(BATCH):
        for t in range(b * steps_per_chunk, (b + 1) * steps_per_chunk):
            idx = pl.ds(t, BATCH, stride=pitch)
            for c in range(LRU_BLOCKS):
                h[c] = a_s[c, idx, :] * h[c] + u_s[c, idx, :]
                u_s[c, idx, :] = h[c]
        for hd in range(MEM_HEADS):
            if b + 1 < BATCH:
                mem_scores(b + 1, hd)
            mem_output(b, hd)
    for c in range(LRU_BLOCKS):
        h_s[:, c * LRU_BLOCK:(c + 1) * LRU_BLOCK] = h[c]

    nb = BATCH // OUT_CHUNKS_A
    for oc in range(OUT_CHUNKS_A):
        for b in range(oc * nb, (oc + 1) * nb):
            rs = slice(b * ts, (b + 1) * ts)
            for c in range(LRU_BLOCKS):
                cs = slice(c * LRU_BLOCK, (c + 1) * LRU_BLOCK)
                cat_s[rs, cs] = (u_s[c, b * pitch:b * pitch + ts, :] * xc_s[c, rs, :]).astype(BF16)
            cat_s[rs, LRU_WIDTH:] = mt_s[:, rs].T.astype(BF16)
        rows_c = slice(oc * nb * ts, (oc + 1) * nb * ts)
        mixed = _dot(cat_s[rows_c, :], wout_ref[...])
        x_c = x_ref[oc * nb:(oc + 1) * nb].reshape(nb * ts, D_MODEL)
        o_ref[oc * nb:(oc + 1) * nb] = (x_c + _rms(mixed, gpost_ref[...])).reshape(nb, ts, D_MODEL)


def _mixer_a(x, gpre, w_in, w_q, conv_w, conv_b, w_gate, b_gate, lam, mkv, mvt, w_out_all, gpost):
    ts, pitch = TS_A, PITCH_A
    rows = BATCH * ts
    return pl.pallas_call(
        _mixer_a_kernel,
        grid=(SEQ // ts,),
        in_specs=[
            pl.BlockSpec((BATCH, ts, D_MODEL), lambda i: (0, i, 0)),
            _const_spec((1, D_MODEL)),
            _const_spec(w_in.shape),
            _const_spec(w_q.shape),
            _const_spec(conv_w.shape),
            _const_spec((1, LRU_WIDTH)),
            _const_spec(w_gate.shape),
            _const_spec(b_gate.shape),
            _const_spec((1, LRU_WIDTH)),
            _layer_spec(mkv.shape, 0),
            _layer_spec(mvt.shape, 0),
            _layer_spec(w_out_all.shape, 0),
            _const_spec((1, D_MODEL)),
        ],
        out_specs=pl.BlockSpec((BATCH, ts, D_MODEL), lambda i: (0, i, 0)),
        out_shape=jax.ShapeDtypeStruct((BATCH, SEQ, D_MODEL), F32),
        scratch_shapes=[
            pltpu.VMEM((LRU_BLOCKS, BATCH * (ts + SUBLANES), LRU_BLOCK), F32),
            pltpu.VMEM((LRU_BLOCKS, rows, LRU_BLOCK), F32),
            pltpu.VMEM((LRU_BLOCKS, BATCH * pitch, LRU_BLOCK), F32),
            pltpu.VMEM((LRU_BLOCKS, BATCH * pitch, LRU_BLOCK), F32),
            pltpu.VMEM((MEM_WIDTH, rows), BF16),
            pltpu.VMEM((2, MEM_HEADS, N_MEM, ts), F32),
            pltpu.VMEM((MEM_WIDTH, rows), F32),
            pltpu.VMEM((rows, D_MODEL), BF16),
            pltpu.VMEM((BATCH, LRU_WIDTH), F32),
        ],
        compiler_params=pltpu.CompilerParams(dimension_semantics=("arbitrary",),
                                             vmem_limit_bytes=VMEM_LIMIT),
        name="mixer_a",
    )(x, gpre.reshape(1, D_MODEL), w_in, w_q, conv_w, conv_b.reshape(1, LRU_WIDTH), w_gate, b_gate,
      lam.reshape(1, LRU_WIDTH), mkv, mvt, w_out_all, gpost.reshape(1, D_MODEL))


def _ffn_body(h_ref, gpre_ref, win_ref, wout_ref, gpost_ref):
    h = h_ref[...]
    hn = _rms(h, gpre_ref[...]).astype(BF16)
    acc = jnp.zeros((h.shape[0], D_MODEL), F32)
    for ck in range(D_FF // FF_CHUNK):
        lo = ck * FF_CHUNK
        g = _dot(hn, win_ref[:, lo:lo + FF_CHUNK].astype(BF16))
        u = _dot(hn, win_ref[:, D_FF + lo:D_FF + lo + FF_CHUNK].astype(BF16))
        act = (g * jax.nn.sigmoid(g) * u).astype(BF16)
        acc = acc + _dot(act, wout_ref[lo:lo + FF_CHUNK, :].astype(BF16))
    return h + _rms(acc, gpost_ref[...])


def _ffn_kernel(h_ref, gpre_ref, win_ref, wout_ref, gpost_ref, o_ref):
    o_ref[...] = _ffn_body(h_ref, gpre_ref, win_ref, wout_ref, gpost_ref)


def _ffn_kv_kernel(h_ref, gpre_ref, win_ref, wout_ref, gpost_ref, gkv_ref, wk_ref, wvt_ref, cos_ref, sin_ref,
                   o_ref, k_ref, vt_ref):
    h_new = _ffn_body(h_ref, gpre_ref, win_ref, wout_ref, gpost_ref)
    o_ref[...] = h_new
    hn = _rms(h_new, gkv_ref[...]).astype(BF16)
    k = _dot(hn, wk_ref[...])
    cos, sin = cos_ref[...], sin_ref[...]
    for j in range(KV_WIDTH // LANES):
        k_ref[:, j * LANES:(j + 1) * LANES] = _rope(k[:, j * LANES:(j + 1) * LANES], cos, sin).astype(BF16)
    vt_ref[0] = _dot_nt(wvt_ref[...], hn).astype(BF16)


def _ffn(h2d, layer, gpre, w_in, w_out, gpost, kv_args=None):
    tokens = h2d.shape[0]
    tm = TM_FFN if kv_args is None else TM_FFN_KV
    row_spec = pl.BlockSpec((tm, D_MODEL), lambda i: (i, 0))
    in_specs = [row_spec, _const_spec((1, D_MODEL)), _layer_spec(w_in.shape, layer),
                _layer_spec(w_out.shape, layer), _const_spec((1, D_MODEL))]
    args = [h2d, gpre.reshape(1, D_MODEL), w_in, w_out, gpost.reshape(1, D_MODEL)]
    params = pltpu.CompilerParams(dimension_semantics=("arbitrary",),
                                  vmem_limit_bytes=VMEM_LIMIT if kv_args is None else VMEM_LIMIT_FFN_KV)
    if kv_args is None:
        return pl.pallas_call(
            _ffn_kernel, grid=(tokens // tm,), in_specs=in_specs, out_specs=row_spec,
            out_shape=jax.ShapeDtypeStruct((tokens, D_MODEL), F32),
            compiler_params=params, name="ffn",
        )(*args)
    gkv, wk, wvt, cos, sin = kv_args
    pos_blocks = SEQ // tm
    table_spec = pl.BlockSpec((tm, LANES), lambda i: (i % pos_blocks, 0))
    k_spec = pl.BlockSpec((tm, KV_WIDTH), lambda i: (i, 0))
    vt_spec = pl.BlockSpec((1, KV_WIDTH, tm), lambda i: (i // pos_blocks, 0, i % pos_blocks))
    return pl.pallas_call(
        _ffn_kv_kernel, grid=(tokens // tm,),
        in_specs=in_specs + [_const_spec((1, D_MODEL)), _const_spec(wk.shape), _const_spec(wvt.shape),
                             table_spec, table_spec],
        out_specs=[row_spec, k_spec, vt_spec],
        out_shape=[jax.ShapeDtypeStruct((tokens, D_MODEL), F32),
                   jax.ShapeDtypeStruct((tokens, KV_WIDTH), BF16),
                   jax.ShapeDtypeStruct((BATCH, KV_WIDTH, SEQ), BF16)],
        compiler_params=params, name="ffn_kv",
    )(*args, gkv.reshape(1, D_MODEL), wk, wvt, cos, sin)


def _mixer_b_kernel(sinks_ref, x_ref, gpre_ref, win_ref, cost_ref, sint_ref, k_ref, vt_ref, mkv_ref, mvt_ref,
                    wout_ref, gpost_ref, o_ref, qt_s, s_s, sm_s, catt_s, bound_s, e_s):
    b = pl.program_id(0)
    j = pl.program_id(1)
    tq = TQ_B
    n_qb = tq // WINDOW
    q_width = SWA_Q_HEADS * HEAD_DIM
    tc = tq // PROJ_CHUNKS_B
    for c in range(PROJ_CHUNKS_B):
        ts_ = slice(c * tc, (c + 1) * tc)
        hn = _rms(x_ref[0, ts_, :], gpre_ref[...]).astype(BF16)
        cos_t, sin_t = cost_ref[:, ts_], sint_ref[:, ts_]
        pt = _dot(hn, win_ref[...]).T
        for hq in range(SWA_Q_HEADS):
            r0 = hq * HEAD_DIM
            x1, x2 = pt[r0:r0 + ROPE_HALF], pt[r0 + ROPE_HALF:r0 + ROPE_DIM]
            head = jnp.concatenate([x1 * cos_t - x2 * sin_t, x2 * cos_t + x1 * sin_t,
                                    pt[r0 + ROPE_DIM:r0 + HEAD_DIM]], axis=0)
            qt_s[r0:r0 + HEAD_DIM, ts_] = (head * ATTN_SCALE).astype(BF16)
        qt_s[q_width:, ts_] = (pt[q_width:] * ATTN_SCALE).astype(BF16)

    @pl.when(jnp.logical_and(b == 0, j == 0))
    def _():
        ci = lax.broadcasted_iota(jnp.int32, (2 * WINDOW, WINDOW), 0)
        qi = lax.broadcasted_iota(jnp.int32, (2 * WINDOW, WINDOW), 1)
        dist = ci - qi
        band = jnp.logical_and(dist > 0, dist <= WINDOW)
        bound_s[0] = jnp.where(band, jnp.inf, NEG_INF)
        bound_s[1] = jnp.where(jnp.logical_and(band, ci >= WINDOW), jnp.inf, NEG_INF)

    def starts(qb):
        start = pl.multiple_of(j * tq + qb * WINDOW, WINDOW)
        return pl.multiple_of(jnp.maximum(start - WINDOW, 0), WINDOW), start

    def swa_scores(qb, kh):
        pstart, start = starts(qb)
        slab = slice((kh // 2) * LANES, (kh // 2 + 1) * LANES)
        band = jnp.concatenate([k_ref[0, pl.ds(pstart, WINDOW), slab], k_ref[0, pl.ds(start, WINDOW), slab]],
                               axis=0)
        qs = slice(qb * WINDOW, (qb + 1) * WINDOW)
        qcat = jnp.concatenate([qt_s[(kh * SWA_GROUP + g) * HEAD_DIM:(kh * SWA_GROUP + g + 1) * HEAD_DIM, qs]
                                for g in range(SWA_GROUP)], axis=1)
        s_s[qb % 2, kh] = _dot(band, _pad_head_rows(qcat, kh % 2))

    def swa_output(qb, kh):
        pstart, start = starts(qb)
        bound = bound_s[jnp.where(j > 0, 0, 1)] if qb == 0 else bound_s[0]
        slot = kh % 2
        rden = []
        for g in range(SWA_GROUP):
            gs = slice(g * WINDOW, (g + 1) * WINDOW)
            s = jnp.minimum(s_s[qb % 2, kh, :, gs], bound)
            sink = jnp.full((1, WINDOW), sinks_ref[kh * SWA_GROUP + g], F32)
            m = jnp.maximum(jnp.max(s, axis=0, keepdims=True), sink)
            e = jnp.exp(s - m)
            rden.append(1.0 / (jnp.sum(e, axis=0, keepdims=True) + jnp.exp(sink - m)))
            e_s[slot, :, gs] = e.astype(BF16)
        rows = slice(kh * HEAD_DIM, (kh + 1) * HEAD_DIM)
        vband = jnp.concatenate([vt_ref[0, rows, pl.ds(pstart, WINDOW)], vt_ref[0, rows, pl.ds(start, WINDOW)]],
                                axis=1)
        ot = _dot(vband, e_s[slot, :, 0:SWA_GROUP * WINDOW])
        for g in range(SWA_GROUP):
            h = kh * SWA_GROUP + g
            catt_s[h * HEAD_DIM:(h + 1) * HEAD_DIM, qb * WINDOW:(qb + 1) * WINDOW] = (
                ot[:, g * WINDOW:(g + 1) * WINDOW] * rden[g]).astype(BF16)

    def mem_scores(h):
        slab = slice((h // 2) * LANES, (h // 2 + 1) * LANES)
        q = qt_s[q_width + h * HEAD_DIM:q_width + (h + 1) * HEAD_DIM, :]
        sm_s[h] = _dot(mkv_ref[b, :, slab], _pad_head_rows(q, h % 2))

    def mem_output(h):
        slot = h % 2
        rden = []
        for g in range(tq // LANES):
            gs = slice(g * LANES, (g + 1) * LANES)
            s = sm_s[h, :, gs]
            e = jnp.exp(s - jnp.max(s, axis=0, keepdims=True))
            rden.append(1.0 / jnp.sum(e, axis=0, keepdims=True))
            e_s[slot, :, gs] = e.astype(BF16)
        ot = _dot(mvt_ref[b, h * HEAD_DIM:(h + 1) * HEAD_DIM, :], e_s[slot]) * jnp.concatenate(rden, axis=1)
        catt_s[q_width + h * HEAD_DIM:q_width + (h + 1) * HEAD_DIM, :] = ot.astype(BF16)

    for kh in range(SWA_KV_HEADS):
        swa_scores(0, kh)
    for qb in range(n_qb):
        for kh in range(SWA_KV_HEADS):
            if qb + 1 < n_qb:
                swa_scores(qb + 1, kh)
            else:
                mem_scores(kh)
            swa_output(qb, kh)
    for h in range(MEM_HEADS):
        mem_output(h)

    for c in range(PROJ_CHUNKS_B):
        ts_ = slice(c * tc, (c + 1) * tc)
        mixed = lax.dot_general(catt_s[:, ts_], wout_ref[...], (((0,), (0,)), ((), ())),
                                preferred_element_type=F32)
        o_ref[0, ts_, :] = x_ref[0, ts_, :] + _rms(mixed, gpost_ref[...])


def _mixer_b(h, sinks, gpre, w_in, cos_t, sin_t, k, vt, mkv, mvt, w_out_all, gpost):
    tq = TQ_B
    grid_spec = pltpu.PrefetchScalarGridSpec(
        num_scalar_prefetch=1,
        grid=(BATCH, SEQ // tq),
        in_specs=[
            pl.BlockSpec((1, tq, D_MODEL), lambda b, j, s: (b, j, 0)),
            _const_spec((1, D_MODEL)),
            _const_spec(w_in.shape),
            pl.BlockSpec((ROPE_HALF, tq), lambda b, j, s: (0, j)),
            pl.BlockSpec((ROPE_HALF, tq), lambda b, j, s: (0, j)),
            pl.BlockSpec((1, SEQ, KV_WIDTH), lambda b, j, s: (b, 0, 0)),
            pl.BlockSpec((1, KV_WIDTH, SEQ), lambda b, j, s: (b, 0, 0)),
            _layer_spec(mkv.shape, 1),
            _layer_spec(mvt.shape, 1),
            _layer_spec(w_out_all.shape, 1),
            _const_spec((1, D_MODEL)),
        ],
        out_specs=pl.BlockSpec((1, tq, D_MODEL), lambda b, j, s: (b, j, 0)),
        scratch_shapes=[
            pltpu.VMEM((D_MODEL, tq), BF16),
            pltpu.VMEM((2, SWA_KV_HEADS, 2 * WINDOW, SWA_GROUP * WINDOW), F32),
            pltpu.VMEM((MEM_HEADS, N_MEM, tq), F32),
            pltpu.VMEM((D_MODEL, tq), BF16),
            pltpu.VMEM((2, 2 * WINDOW, WINDOW), F32),
            pltpu.VMEM((2, N_MEM, tq), BF16),
        ],
    )
    return pl.pallas_call(
        _mixer_b_kernel,
        grid_spec=grid_spec,
        out_shape=jax.ShapeDtypeStruct((BATCH, SEQ, D_MODEL), F32),
        compiler_params=pltpu.CompilerParams(dimension_semantics=("arbitrary", "arbitrary"),
                                             vmem_limit_bytes=VMEM_LIMIT),
        name="mixer_b",
    )(sinks, h, gpre.reshape(1, D_MODEL), w_in, cos_t, sin_t, k, vt, mkv, mvt, w_out_all,
      gpost.reshape(1, D_MODEL))


def _rope_tables():
    inv_freq = 1.0 / (ROPE_THETA ** (jnp.arange(0, ROPE_DIM, 2, dtype=F32) / ROPE_DIM))
    ang = jnp.arange(SEQ, dtype=F32)[:, None] * inv_freq[None, :]
    c, s = jnp.cos(ang), jnp.sin(ang)
    rest = HEAD_DIM - ROPE_DIM
    cos_h = jnp.concatenate([c, c, jnp.ones((SEQ, rest), F32)], axis=-1)
    sin_h = jnp.concatenate([-s, s, jnp.zeros((SEQ, rest), F32)], axis=-1)
    reps = LANES // HEAD_DIM
    return jnp.tile(cos_h, (1, reps)), jnp.tile(sin_h, (1, reps)), c.T, s.T


def kernel(x, mem, norm_mix_pre, norm_mix_post, norm_ffn_pre, norm_ffn_post, mem_norm, w_mem_kv, w_in_a,
           conv_w, conv_b, w_gate_r, b_gate_r, w_gate_i, b_gate_i, lru_lambda, norm_kv, w_kv_shared, w_in_b,
           sinks, w_out, w_ffn_in, w_ffn_out):
    tokens = BATCH * SEQ
    cos, sin, cos_t, sin_t = _rope_tables()
    w_gate = jnp.concatenate([w_gate_r[0], w_gate_i[0]], axis=-1).astype(BF16)
    b_gate = jnp.concatenate([b_gate_r[0].reshape(LRU_BLOCKS, 1, LRU_BLOCK),
                              b_gate_i[0].reshape(LRU_BLOCKS, 1, LRU_BLOCK)], axis=-1)

    w_out_bf = w_out.astype(BF16)
    w_mem_vt = jnp.swapaxes(w_mem_kv[:, :, MEM_WIDTH:], 1, 2).astype(BF16)
    mkv, mvt = _memkv(mem, mem_norm, w_mem_kv.astype(BF16), w_mem_vt)

    h = _mixer_a(x, norm_mix_pre[0], w_in_a[0, :, :2 * LRU_WIDTH].astype(BF16),
                 w_in_a[0, :, 2 * LRU_WIDTH:].astype(BF16), conv_w[0], conv_b[0], w_gate, b_gate,
                 lru_lambda[0], mkv, mvt, w_out_bf, norm_mix_post[0])
    h2d, k, vt = _ffn(h.reshape(tokens, D_MODEL), 0, norm_ffn_pre[0], w_ffn_in, w_ffn_out, norm_ffn_post[0],
                      kv_args=(norm_kv, w_kv_shared[:, :KV_WIDTH].astype(BF16),
                               w_kv_shared[:, KV_WIDTH:].T.astype(BF16), cos, sin))

    h = _mixer_b(h2d.reshape(BATCH, SEQ, D_MODEL), sinks[0], norm_mix_pre[1], w_in_b[0].astype(BF16),
                 cos_t, sin_t, k.reshape(BATCH, SEQ, KV_WIDTH), vt, mkv, mvt, w_out_bf, norm_mix_post[1])
    h2d = _ffn(h.reshape(tokens, D_MODEL), 1, norm_ffn_pre[1], w_ffn_in, w_ffn_out, norm_ffn_post[1])
    return h2d.reshape(BATCH, SEQ, D_MODEL)
```

```python
import math

import jax
import jax.numpy as jnp
from jax import lax
from jax.experimental import pallas as pl
from jax.experimental.pallas import tpu as pltpu

D_MODEL = 1024
BATCH = 8
SEQ = 4096
HEAD_DIM = 64
MEM_HEADS = 4
MEM_WIDTH = MEM_HEADS * HEAD_DIM
N_MEM = 256
LRU_WIDTH = D_MODEL - MEM_WIDTH
LRU_BLOCKS = 6
LRU_BLOCK = LRU_WIDTH // LRU_BLOCKS
CONV_WIDTH = 4
LRU_C = 8.0
SWA_Q_HEADS = 12
SWA_KV_HEADS = 4
SWA_GROUP = SWA_Q_HEADS // SWA_KV_HEADS
KV_WIDTH = SWA_KV_HEADS * HEAD_DIM
WINDOW = 128
ROPE_DIM = HEAD_DIM // 4
ROPE_HALF = ROPE_DIM // 2
ROPE_THETA = 500000.0
D_FF = 2816
EPS = 1e-6
NEG_INF = -1e30
ATTN_SCALE = HEAD_DIM ** -0.5

LANES = 128
SUBLANES = 8
VMEM_LIMIT = 56 * 1024 * 1024
VMEM_LIMIT_FFN_KV = 62 * 1024 * 1024

TS_A = 128
PITCH_A = TS_A + 4
TM_FFN = 1024
TM_FFN_KV = 1024
FF_CHUNK = 256
KV_SLICES = 4
TQ_B = 512
PROJ_CHUNKS_B = 2
OUT_CHUNKS_A = 4

BF16 = jnp.bfloat16
F32 = jnp.float32


def _rms(x, g):
    ms = jnp.mean(x * x, axis=-1, keepdims=True)
    return x * lax.rsqrt(ms + EPS) * g


def _dot(a, b):
    return jnp.dot(a, b, preferred_element_type=F32)


def _dot_nt(a, b):
    return lax.dot_general(a, b, (((1,), (1,)), ((), ())), preferred_element_type=F32)


def _gelu_tanh(x):
    k1 = -2.0 * math.sqrt(2.0 / math.pi) * math.log2(math.e)
    return x / (1.0 + jnp.exp2(x * (k1 + (k1 * 0.044715) * (x * x))))


def _rope(x, cos, sin):
    lane = lax.broadcasted_iota(jnp.int32, x.shape, 1) & (HEAD_DIM - 1)
    partner = jnp.where(lane < ROPE_HALF,
                        pltpu.roll(x, LANES - ROPE_HALF, 1),
                        pltpu.roll(x, ROPE_HALF, 1))
    return x * cos + partner * sin


def _memkv_kernel(mem_ref, g_ref, w_ref, wvt_ref, o_ref, ovt_ref):
    mem_n = _rms(mem_ref[0], g_ref[...]).astype(BF16)
    for l in range(2):
        o_ref[l, 0] = _dot(mem_n, w_ref[l]).astype(BF16)
        ovt_ref[l, 0] = _dot_nt(wvt_ref[l], mem_n).astype(BF16)


def _memkv(mem, mem_norm, w_mem_kv, w_mem_vt):
    return pl.pallas_call(
        _memkv_kernel,
        grid=(BATCH,),
        in_specs=[
            pl.BlockSpec((1, N_MEM, D_MODEL), lambda b: (b, 0, 0)),
            pl.BlockSpec((1, D_MODEL), lambda b: (0, 0)),
            pl.BlockSpec((2, D_MODEL, 2 * MEM_WIDTH), lambda b: (0, 0, 0)),
            pl.BlockSpec((2, MEM_WIDTH, D_MODEL), lambda b: (0, 0, 0)),
        ],
        out_specs=[pl.BlockSpec((2, 1, N_MEM, 2 * MEM_WIDTH), lambda b: (0, b, 0, 0)),
                   pl.BlockSpec((2, 1, MEM_WIDTH, N_MEM), lambda b: (0, b, 0, 0))],
        out_shape=[jax.ShapeDtypeStruct((2, BATCH, N_MEM, 2 * MEM_WIDTH), BF16),
                   jax.ShapeDtypeStruct((2, BATCH, MEM_WIDTH, N_MEM), BF16)],
        compiler_params=pltpu.CompilerParams(dimension_semantics=("arbitrary",),
                                             vmem_limit_bytes=VMEM_LIMIT),
        name="memkv",
    )(mem, mem_norm.reshape(1, D_MODEL), w_mem_kv, w_mem_vt)


def _const_spec(shape):
    nd = len(shape)
    return pl.BlockSpec(shape, lambda *_: (0,) * nd, pipeline_mode=pl.Buffered(1))


def _layer_spec(stacked_shape, layer):
    nd = len(stacked_shape) - 1
    return pl.BlockSpec((None,) + tuple(stacked_shape[1:]), lambda *_: (layer,) + (0,) * nd,
                        pipeline_mode=pl.Buffered(1))


def _pad_head_rows(q, half):
    z = jnp.zeros_like(q)
    return jnp.concatenate([z, q] if half else [q, z], axis=0)


def _mixer_a_kernel(x_ref, gpre_ref, win_ref, wq_ref, convw_ref, convb_ref, wg_ref, bg_ref, lam_ref,
                    mkv_ref, mvt_ref, wout_ref, gpost_ref, o_ref,
                    xr_s, xc_s, a_s, u_s, qt_s, sm_s, mt_s, cat_s, h_s):
    i = pl.program_id(0)
    ts, pitch = TS_A, PITCH_A
    rows = BATCH * ts

    @pl.when(i == 0)
    def _():
        h_s[...] = jnp.zeros_like(h_s)
        xr_s[...] = jnp.zeros_like(xr_s)

    hist = ts + SUBLANES
    nb = BATCH // OUT_CHUNKS_A
    hn_parts = []
    for rc in range(OUT_CHUNKS_A):
        x_c = x_ref[rc * nb:(rc + 1) * nb].reshape(nb * ts, D_MODEL)
        hn_c = _rms(x_c, gpre_ref[...]).astype(BF16)
        hn_parts.append(hn_c)
        xr = _dot(hn_c, win_ref[:, 0:LRU_WIDTH])
        for c in range(LRU_BLOCKS):
            cs = slice(c * LRU_BLOCK, (c + 1) * LRU_BLOCK)
            for bb in range(nb):
                b = rc * nb + bb
                xr_s[c, b * hist + SUBLANES:(b + 1) * hist, :] = xr[bb * ts:(bb + 1) * ts, cs]
    hn = jnp.concatenate(hn_parts, axis=0)

    qt_s[...] = (_dot(hn, wq_ref[...]) * (ATTN_SCALE * math.log2(math.e))).T.astype(BF16)

    for c in range(LRU_BLOCKS):
        cs = slice(c * LRU_BLOCK, (c + 1) * LRU_BLOCK)
        for b in range(BATCH):
            acc = jnp.broadcast_to(convb_ref[:, cs], (ts, LRU_BLOCK))
            for k in range(CONV_WIDTH):
                off = b * hist + SUBLANES - (CONV_WIDTH - 1) + k
                acc = acc + xr_s[c, off:off + ts, :] * convw_ref[k:k + 1, cs]
            xc_s[c, b * ts:(b + 1) * ts, :] = acc
            xr_s[c, b * hist:b * hist + SUBLANES, :] = xr_s[c, b * hist + ts:(b + 1) * hist, :]

    lam = lam_ref[...]
    log_sig = jnp.minimum(lam, 0.0) - jnp.log1p(jnp.exp(-jnp.abs(lam)))
    ls_c = (LRU_C * math.log2(math.e)) * log_sig
    tpos = lax.broadcasted_iota(jnp.int32, (rows, 1), 0) & (ts - 1)
    seq_start = jnp.logical_and(tpos == 0, i == 0)
    for c in range(LRU_BLOCKS):
        cs = slice(c * LRU_BLOCK, (c + 1) * LRU_BLOCK)
        xc = xc_s[c]
        g = _dot(xc.astype(BF16), wg_ref[c]) + bg_ref[c]
        r = jax.nn.sigmoid(g[:, :LRU_BLOCK])
        ig = jax.nn.sigmoid(g[:, LRU_BLOCK:])
        a = jnp.exp2(r * ls_c[:, cs])
        t = 1.0 - a * a
        mult = jnp.where(t > 0.0, t * lax.rsqrt(t), 0.0)
        mult = jnp.where(seq_start, 1.0, mult)
        u = mult * (ig * xc)
        for b in range(BATCH):
            a_s[c, b * pitch:b * pitch + ts, :] = a[b * ts:(b + 1) * ts]
            u_s[c, b * pitch:b * pitch + ts, :] = u[b * ts:(b + 1) * ts]

        if c % 2 == 1:
            gs = slice((c - 1) * LRU_BLOCK, (c + 1) * LRU_BLOCK)
            gate = _gelu_tanh(_dot(hn, win_ref[:, LRU_WIDTH + gs.start:LRU_WIDTH + gs.stop]))
            xc_s[c - 1] = gate[:, :LRU_BLOCK]
            xc_s[c] = gate[:, LRU_BLOCK:]

    def mem_scores(b, h):
        slab = slice((h // 2) * LANES, (h // 2 + 1) * LANES)
        q = qt_s[h * HEAD_DIM:(h + 1) * HEAD_DIM, b * ts:(b + 1) * ts]
        sm_s[b % 2, h] = _dot(mkv_ref[b, :, slab], _pad_head_rows(q, h % 2))

    def mem_output(b, h):
        s = sm_s[b % 2, h]
        m = jnp.max(s, axis=0, keepdims=True)
        e = jnp.exp2(s - m)
        den = jnp.sum(e, axis=0, keepdims=True)
        ot = _dot(mvt_ref[b, h * HEAD_DIM:(h + 1) * HEAD_DIM, :], e.astype(BF16)) * (1.0 / den)
        mt_s[h * HEAD_DIM:(h + 1) * HEAD_DIM, b * ts:(b + 1) * ts] = ot

    h = [h_s[:, c * LRU_BLOCK:(c + 1) * LRU_BLOCK] for c in range(LRU_BLOCKS)]
    steps_per_chunk = ts // BATCH
    for hd in range(MEM_HEADS):
        mem_scores(0, hd)
    for b in range(BATCH):
        for t in range(b * steps_per_chunk, (b + 1) * steps_per_chunk):
            idx = pl.ds(t, BATCH, stride=pitch)
            for c in range(LRU_BLOCKS):
                h[c] = a_s[c, idx, :] * h[c] + u_s[c, idx, :]
                u_s[c, idx, :] = h[c]
        for hd in range(MEM_HEADS):
            if b + 1 < BATCH:
                mem_scores(b + 1, hd)
            mem_output(b, hd)
    for c in range(LRU_BLOCKS):
        h_s[:, c * LRU_BLOCK:(c + 1) * LRU_BLOCK] = h[c]

    nb = BATCH // OUT_CHUNKS_A
    for oc in range(OUT_CHUNKS_A):
        for b in range(oc * nb, (oc + 1) * nb):
            rs = slice(b * ts, (b + 1) * ts)
            for c in range(LRU_BLOCKS):
                cs = slice(c * LRU_BLOCK, (c + 1) * LRU_BLOCK)
                cat_s[rs, cs] = (u_s[c, b * pitch:b * pitch + ts, :] * xc_s[c, rs, :]).astype(BF16)
            cat_s[rs, LRU_WIDTH:] = mt_s[:, rs].T.astype(BF16)
        rows_c = slice(oc * nb * ts, (oc + 1) * nb * ts)
        mixed = _dot(cat_s[rows_c, :], wout_ref[...])
        x_c = x_ref[oc * nb:(oc + 1) * nb].reshape(nb * ts, D_MODEL)
        o_ref[oc * nb:(oc + 1) * nb] = (x_c + _rms(mixed, gpost_ref[...])).reshape(nb, ts, D_MODEL)


def _mixer_a(x, gpre, w_in, w_q, conv_w, conv_b, w_gate, b_gate, lam, mkv, mvt, w_out_all, gpost):
    ts, pitch = TS_A, PITCH_A
    rows = BATCH * ts
    return pl.pallas_call(
        _mixer_a_kernel,
        grid=(SEQ // ts,),
        in_specs=[
            pl.BlockSpec((BATCH, ts, D_MODEL), lambda i: (0, i, 0)),
            _const_spec((1, D_MODEL)),
            _const_spec(w_in.shape),
            _const_spec(w_q.shape),
            _const_spec(conv_w.shape),
            _const_spec((1, LRU_WIDTH)),
            _const_spec(w_gate.shape),
            _const_spec(b_gate.shape),
            _const_spec((1, LRU_WIDTH)),
            _layer_spec(mkv.shape, 0),
            _layer_spec(mvt.shape, 0),
            _layer_spec(w_out_all.shape, 0),
            _const_spec((1, D_MODEL)),
        ],
        out_specs=pl.BlockSpec((BATCH, ts, D_MODEL), lambda i: (0, i, 0)),
        out_shape=jax.ShapeDtypeStruct((BATCH, SEQ, D_MODEL), F32),
        scratch_shapes=[
            pltpu.VMEM((LRU_BLOCKS, BATCH * (ts + SUBLANES), LRU_BLOCK), F32),
            pltpu.VMEM((LRU_BLOCKS, rows, LRU_BLOCK), F32),
            pltpu.VMEM((LRU_BLOCKS, BATCH * pitch, LRU_BLOCK), F32),
            pltpu.VMEM((LRU_BLOCKS, BATCH * pitch, LRU_BLOCK), F32),
            pltpu.VMEM((MEM_WIDTH, rows), BF16),
            pltpu.VMEM((2, MEM_HEADS, N_MEM, ts), F32),
            pltpu.VMEM((MEM_WIDTH, rows), F32),
            pltpu.VMEM((rows, D_MODEL), BF16),
            pltpu.VMEM((BATCH, LRU_WIDTH), F32),
        ],
        compiler_params=pltpu.CompilerParams(dimension_semantics=("arbitrary",),
                                             vmem_limit_bytes=VMEM_LIMIT),
        name="mixer_a",
    )(x, gpre.reshape(1, D_MODEL), w_in, w_q, conv_w, conv_b.reshape(1, LRU_WIDTH), w_gate, b_gate,
      lam.reshape(1, LRU_WIDTH), mkv, mvt, w_out_all, gpost.reshape(1, D_MODEL))


def _ffn_body(h_ref, gpre_ref, win_ref, wout_ref, gpost_ref, n_slices=1):
    tm = h_ref.shape[0]
    hn = _rms(h_ref[...], gpre_ref[...]).astype(BF16)
    acc = jnp.zeros((tm, D_MODEL), F32)
    n_ck = D_FF // FF_CHUNK
    out = []
    for ck in range(n_ck):
        lo = ck * FF_CHUNK
        g = _dot(hn, win_ref[:, lo:lo + FF_CHUNK].astype(BF16))
        u = _dot(hn, win_ref[:, D_FF + lo:D_FF + lo + FF_CHUNK].astype(BF16))
        act = (g * jax.nn.sigmoid(g) * u).astype(BF16)
        w_o = wout_ref[lo:lo + FF_CHUNK, :].astype(BF16)
        if ck + 1 < n_ck:
            acc = acc + _dot(act, w_o)
        else:
            sm = tm // n_slices
            for si in range(n_slices):
                rs = slice(si * sm, (si + 1) * sm)
                out.append((rs, h_ref[rs, :] + _rms(acc[rs] + _dot(act[rs], w_o), gpost_ref[...])))
    return out


def _ffn_kernel(h_ref, gpre_ref, win_ref, wout_ref, gpost_ref, o_ref):
    for rs, h_new in _ffn_body(h_ref, gpre_ref, win_ref, wout_ref, gpost_ref):
        o_ref[rs, :] = h_new


def _ffn_kv_kernel(h_ref, gpre_ref, win_ref, wout_ref, gpost_ref, gkv_ref, wk_ref, wvt_ref, cos_ref, sin_ref,
                   o_ref, k_ref, vt_ref):
    for rs, h_new in _ffn_body(h_ref, gpre_ref, win_ref, wout_ref, gpost_ref, n_slices=KV_SLICES):
        o_ref[rs, :] = h_new
        hn = _rms(h_new, gkv_ref[...]).astype(BF16)
        k = _dot(hn, wk_ref[...])
        cos, sin = cos_ref[rs, :], sin_ref[rs, :]
        for j in range(KV_WIDTH // LANES):
            k_ref[rs, j * LANES:(j + 1) * LANES] = _rope(k[:, j * LANES:(j + 1) * LANES], cos, sin).astype(BF16)
        vt_ref[0, :, rs] = _dot_nt(wvt_ref[...], hn).astype(BF16)


def _ffn(h2d, layer, gpre, w_in, w_out, gpost, kv_args=None):
    tokens = h2d.shape[0]
    tm = TM_FFN if kv_args is None else TM_FFN_KV
    row_spec = pl.BlockSpec((tm, D_MODEL), lambda i: (i, 0))
    in_specs = [row_spec, _const_spec((1, D_MODEL)), _layer_spec(w_in.shape, layer),
                _layer_spec(w_out.shape, layer), _const_spec((1, D_MODEL))]
    args = [h2d, gpre.reshape(1, D_MODEL), w_in, w_out, gpost.reshape(1, D_MODEL)]
    params = pltpu.CompilerParams(dimension_semantics=("arbitrary",),
                                  vmem_limit_bytes=VMEM_LIMIT if kv_args is None else VMEM_LIMIT_FFN_KV)
    if kv_args is None:
        return pl.pallas_call(
            _ffn_kernel, grid=(tokens // tm,), in_specs=in_specs, out_specs=row_spec,
            out_shape=jax.ShapeDtypeStruct((tokens, D_MODEL), F32),
            compiler_params=params, name="ffn",
        )(*args)
    gkv, wk, wvt, cos, sin = kv_args
    pos_blocks = SEQ // tm
    table_spec = pl.BlockSpec((tm, LANES), lambda i: (i % pos_blocks, 0))
    k_spec = pl.BlockSpec((tm, KV_WIDTH), lambda i: (i, 0))
    vt_spec = pl.BlockSpec((1, KV_WIDTH, tm), lambda i: (i // pos_blocks, 0, i % pos_blocks))
    return pl.pallas_call(
        _ffn_kv_kernel, grid=(tokens // tm,),
        in_specs=in_specs + [_const_spec((1, D_MODEL)), _const_spec(wk.shape), _const_spec(wvt.shape),
                             table_spec, table_spec],
        out_specs=[row_spec, k_spec, vt_spec],
        out_shape=[jax.ShapeDtypeStruct((tokens, D_MODEL), F32),
                   jax.ShapeDtypeStruct((tokens, KV_WIDTH), BF16),
                   jax.ShapeDtypeStruct((BATCH, KV_WIDTH, SEQ), BF16)],
        compiler_params=params, name="ffn_kv",
    )(*args, gkv.reshape(1, D_MODEL), wk, wvt, cos, sin)


def _mixer_b_kernel(sinks_ref, x_ref, gpre_ref, win_ref, cost_ref, sint_ref, k_ref, vt_ref, mkv_ref, mvt_ref,
                    wout_ref, gpost_ref, o_ref, qt_s, s_s, sm_s, catt_s, bound_s, e_s):
    b = pl.program_id(0)
    j = pl.program_id(1)
    tq = TQ_B
    n_qb = tq // WINDOW
    q_width = SWA_Q_HEADS * HEAD_DIM
    tc = tq // PROJ_CHUNKS_B
    for c in range(PROJ_CHUNKS_B):
        ts_ = slice(c * tc, (c + 1) * tc)
        hn = _rms(x_ref[0, ts_, :], gpre_ref[...]).astype(BF16)
        cos_t, sin_t = cost_ref[:, ts_], sint_ref[:, ts_]
        pt = _dot(hn, win_ref[...]).T
        for hq in range(SWA_Q_HEADS):
            r0 = hq * HEAD_DIM
            x1, x2 = pt[r0:r0 + ROPE_HALF], pt[r0 + ROPE_HALF:r0 + ROPE_DIM]
            head = jnp.concatenate([x1 * cos_t - x2 * sin_t, x2 * cos_t + x1 * sin_t,
                                    pt[r0 + ROPE_DIM:r0 + HEAD_DIM]], axis=0)
            qt_s[r0:r0 + HEAD_DIM, ts_] = (head * ATTN_SCALE).astype(BF16)
        qt_s[q_width:, ts_] = (pt[q_width:] * ATTN_SCALE).astype(BF16)

    @pl.when(jnp.logical_and(b == 0, j == 0))
    def _():
        ci = lax.broadcasted_iota(jnp.int32, (2 * WINDOW, WINDOW), 0)
        qi = lax.broadcasted_iota(jnp.int32, (2 * WINDOW, WINDOW), 1)
        dist = ci - qi
        band = jnp.logical_and(dist > 0, dist <= WINDOW)
        bound_s[0] = jnp.where(band, jnp.inf, NEG_INF)
        bound_s[1] = jnp.where(jnp.logical_and(band, ci >= WINDOW), jnp.inf, NEG_INF)

    def starts(qb):
        start = pl.multiple_of(j * tq + qb * WINDOW, WINDOW)
        return pl.multiple_of(jnp.maximum(start - WINDOW, 0), WINDOW), start

    def swa_scores(qb, kh):
        pstart, start = starts(qb)
        slab = slice((kh // 2) * LANES, (kh // 2 + 1) * LANES)
        band = jnp.concatenate([k_ref[0, pl.ds(pstart, WINDOW), slab], k_ref[0, pl.ds(start, WINDOW), slab]],
                               axis=0)
        qs = slice(qb * WINDOW, (qb + 1) * WINDOW)
        qcat = jnp.concatenate([qt_s[(kh * SWA_GROUP + g) * HEAD_DIM:(kh * SWA_GROUP + g + 1) * HEAD_DIM, qs]
                                for g in range(SWA_GROUP)], axis=1)
        s_s[qb % 2, kh] = _dot(band, _pad_head_rows(qcat, kh % 2))

    def swa_output(qb, kh):
        pstart, start = starts(qb)
        bound = bound_s[jnp.where(j > 0, 0, 1)] if qb == 0 else bound_s[0]
        slot = kh % 2
        rden = []
        for g in range(SWA_GROUP):
            gs = slice(g * WINDOW, (g + 1) * WINDOW)
            s = jnp.minimum(s_s[qb % 2, kh, :, gs], bound)
            sink = jnp.full((1, WINDOW), sinks_ref[kh * SWA_GROUP + g], F32)
            m = jnp.maximum(jnp.max(s, axis=0, keepdims=True), sink)
            e = jnp.exp(s - m)
            rden.append(1.0 / (jnp.sum(e, axis=0, keepdims=True) + jnp.exp(sink - m)))
            e_s[slot, :, gs] = e.astype(BF16)
        rows = slice(kh * HEAD_DIM, (kh + 1) * HEAD_DIM)
        vband = jnp.concatenate([vt_ref[0, rows, pl.ds(pstart, WINDOW)], vt_ref[0, rows, pl.ds(start, WINDOW)]],
                                axis=1)
        ot = _dot(vband, e_s[slot, :, 0:SWA_GROUP * WINDOW])
        for g in range(SWA_GROUP):
            h = kh * SWA_GROUP + g
            catt_s[h * HEAD_DIM:(h + 1) * HEAD_DIM, qb * WINDOW:(qb + 1) * WINDOW] = (
                ot[:, g * WINDOW:(g + 1) * WINDOW] * rden[g]).astype(BF16)

    def mem_scores(h):
        slab = slice((h // 2) * LANES, (h // 2 + 1) * LANES)
        q = qt_s[q_width + h * HEAD_DIM:q_width + (h + 1) * HEAD_DIM, :]
        sm_s[h] = _dot(mkv_ref[b, :, slab], _pad_head_rows(q, h % 2))

    def mem_output(h):
        slot = h % 2
        rden = []
        for g in range(tq // LANES):
            gs = slice(g * LANES, (g + 1) * LANES)
            s = sm_s[h, :, gs]
            e = jnp.exp(s - jnp.max(s, axis=0, keepdims=True))
            rden.append(1.0 / jnp.sum(e, axis=0, keepdims=True))
            e_s[slot, :, gs] = e.astype(BF16)
        ot = _dot(mvt_ref[b, h * HEAD_DIM:(h + 1) * HEAD_DIM, :], e_s[slot]) * jnp.concatenate(rden, axis=1)
        catt_s[q_width + h * HEAD_DIM:q_width + (h + 1) * HEAD_DIM, :] = ot.astype(BF16)

    for kh in range(SWA_KV_HEADS):
        swa_scores(0, kh)
    for qb in range(n_qb):
        for kh in range(SWA_KV_HEADS):
            if qb + 1 < n_qb:
                swa_scores(qb + 1, kh)
            else:
                mem_scores(kh)
            swa_output(qb, kh)
    for h in range(MEM_HEADS):
        mem_output(h)

    for c in range(PROJ_CHUNKS_B):
        ts_ = slice(c * tc, (c + 1) * tc)
        mixed = lax.dot_general(catt_s[:, ts_], wout_ref[...], (((0,), (0,)), ((), ())),
                                preferred_element_type=F32)
        o_ref[0, ts_, :] = x_ref[0, ts_, :] + _rms(mixed, gpost_ref[...])


def _mixer_b(h, sinks, gpre, w_in, cos_t, sin_t, k, vt, mkv, mvt, w_out_all, gpost):
    tq = TQ_B
    grid_spec = pltpu.PrefetchScalarGridSpec(
        num_scalar_prefetch=1,
        grid=(BATCH, SEQ // tq),
        in_specs=[
            pl.BlockSpec((1, tq, D_MODEL), lambda b, j, s: (b, j, 0)),
            _const_spec((1, D_MODEL)),
            _const_spec(w_in.shape),
            pl.BlockSpec((ROPE_HALF, tq), lambda b, j, s: (0, j)),
            pl.BlockSpec((ROPE_HALF, tq), lambda b, j, s: (0, j)),
            pl.BlockSpec((1, SEQ, KV_WIDTH), lambda b, j, s: (b, 0, 0)),
            pl.BlockSpec((1, KV_WIDTH, SEQ), lambda b, j, s: (b, 0, 0)),
            _layer_spec(mkv.shape, 1),
            _layer_spec(mvt.shape, 1),
            _layer_spec(w_out_all.shape, 1),
            _const_spec((1, D_MODEL)),
        ],
        out_specs=pl.BlockSpec((1, tq, D_MODEL), lambda b, j, s: (b, j, 0)),
        scratch_shapes=[
            pltpu.VMEM((D_MODEL, tq), BF16),
            pltpu.VMEM((2, SWA_KV_HEADS, 2 * WINDOW, SWA_GROUP * WINDOW), F32),
            pltpu.VMEM((MEM_HEADS, N_MEM, tq), F32),
            pltpu.VMEM((D_MODEL, tq), BF16),
            pltpu.VMEM((2, 2 * WINDOW, WINDOW), F32),
            pltpu.VMEM((2, N_MEM, tq), BF16),
        ],
    )
    return pl.pallas_call(
        _mixer_b_kernel,
        grid_spec=grid_spec,
        out_shape=jax.ShapeDtypeStruct((BATCH, SEQ, D_MODEL), F32),
        compiler_params=pltpu.CompilerParams(dimension_semantics=("arbitrary", "arbitrary"),
                                             vmem_limit_bytes=VMEM_LIMIT),
        name="mixer_b",
    )(sinks, h, gpre.reshape(1, D_MODEL), w_in, cos_t, sin_t, k, vt, mkv, mvt, w_out_all,
      gpost.reshape(1, D_MODEL))


def _rope_tables():
    inv_freq = 1.0 / (ROPE_THETA ** (jnp.arange(0, ROPE_DIM, 2, dtype=F32) / ROPE_DIM))
    ang = jnp.arange(SEQ, dtype=F32)[:, None] * inv_freq[None, :]
    c, s = jnp.cos(ang), jnp.sin(ang)
    rest = HEAD_DIM - ROPE_DIM
    cos_h = jnp.concatenate([c, c, jnp.ones((SEQ, rest), F32)], axis=-1)
    sin_h = jnp.concatenate([-s, s, jnp.zeros((SEQ, rest), F32)], axis=-1)
    reps = LANES // HEAD_DIM
    return jnp.tile(cos_h, (1, reps)), jnp.tile(sin_h, (1, reps)), c.T, s.T


def kernel(x, mem, norm_mix_pre, norm_mix_post, norm_ffn_pre, norm_ffn_post, mem_norm, w_mem_kv, w_in_a,
           conv_w, conv_b, w_gate_r, b_gate_r, w_gate_i, b_gate_i, lru_lambda, norm_kv, w_kv_shared, w_in_b,
           sinks, w_out, w_ffn_in, w_ffn_out):
    tokens = BATCH * SEQ
    cos, sin, cos_t, sin_t = _rope_tables()
    w_gate = jnp.concatenate([w_gate_r[0], w_gate_i[0]], axis=-1).astype(BF16)
    b_gate = jnp.concatenate([b_gate_r[0].reshape(LRU_BLOCKS, 1, LRU_BLOCK),
                              b_gate_i[0].reshape(LRU_BLOCKS, 1, LRU_BLOCK)], axis=-1)

    w_out_bf = w_out.astype(BF16)
    w_mem_vt = jnp.swapaxes(w_mem_kv[:, :, MEM_WIDTH:], 1, 2).astype(BF16)
    mkv, mvt = _memkv(mem, mem_norm, w_mem_kv.astype(BF16), w_mem_vt)

    h = _mixer_a(x, norm_mix_pre[0], w_in_a[0, :, :2 * LRU_WIDTH].astype(BF16),
                 w_in_a[0, :, 2 * LRU_WIDTH:].astype(BF16), conv_w[0], conv_b[0], w_gate, b_gate,
                 lru_lambda[0], mkv, mvt, w_out_bf, norm_mix_post[0])
    h2d, k, vt = _ffn(h.reshape(tokens, D_MODEL), 0, norm_ffn_pre[0], w_ffn_in, w_ffn_out, norm_ffn_post[0],
                      kv_args=(norm_kv, w_kv_shared[:, :KV_WIDTH].astype(BF16),
                               w_kv_shared[:, KV_WIDTH:].T.astype(BF16), cos, sin))

    h = _mixer_b(h2d.reshape(BATCH, SEQ, D_MODEL), sinks[0], norm_mix_pre[1], w_in_b[0].astype(BF16),
                 cos_t, sin_t, k.reshape(BATCH, SEQ, KV_WIDTH), vt, mkv, mvt, w_out_bf, norm_mix_post[1])
    h2d = _ffn(h.reshape(tokens, D_MODEL), 1, norm_ffn_pre[1], w_ffn_in, w_ffn_out, norm_ffn_post[1])
    return h2d.reshape(BATCH, SEQ, D_MODEL)
```
